```python
import math
import jax, jax.numpy as jnp
from jax import lax
import numpy as np

D_MODEL = 1024
BATCH = 8
SEQ = 2048
DEPTH = 4
DEC_BATCH = 128
DEC_SEQ = 8
PAST_LEN = 8192
PAGE_SIZE = 128

N_EVEN = (DEPTH + 1) // 2
N_ODD = DEPTH // 2
ALPHA = (2.0 * DEPTH) ** 0.25
BETA = (8.0 * DEPTH) ** -0.25
D_FF = 2816
ROPE_BASE = 10000.0
MLA_HEADS = 8
MLA_Q_RANK = 256
MLA_KV_RANK = 128
MLA_NOPE = 64
MLA_ROPE = 32
MLA_V = 64
Q_BLOCK = 128
RET_HEADS = 4
RET_DK = 64
RET_DV = 128
RET_CHUNK = 64
GLA_HEADS = 4
GLA_DK = 64
GLA_DV = 128
GLA_GATE_RANK = 16
GLA_TAU = 16.0
GLA_CHUNK = 64
SSM_HEADS = 16
SSM_HEADDIM = 64
SSM_GROUPS = 2
SSM_DSTATE = 128
SSM_CONV = 4
SSM_CHUNK = 64
SSM_INNER = SSM_HEADS * SSM_HEADDIM
SSM_CONV_DIM = SSM_INNER + 2 * SSM_GROUPS * SSM_DSTATE

EVEN_SPLITS = (MLA_Q_RANK, MLA_KV_RANK, MLA_ROPE, RET_HEADS * RET_DK, RET_HEADS * RET_DK, RET_HEADS * RET_DV, RET_HEADS * RET_DV)
EVEN_IN = sum(EVEN_SPLITS)
EVEN_MIX = MLA_HEADS * MLA_V + RET_HEADS * RET_DV
ODD_SPLITS = (GLA_HEADS * GLA_DK, GLA_HEADS * GLA_DK, GLA_HEADS * GLA_DV, GLA_GATE_RANK, GLA_HEADS * GLA_DV, SSM_INNER, SSM_CONV_DIM, SSM_HEADS)
ODD_IN = sum(ODD_SPLITS)
ODD_MIX = GLA_HEADS * GLA_DV + SSM_INNER

kernel_name = 'hybrid_mla_retnet_gla_ssd_step'


def layer_norm(x, g, b, eps=1e-5):
    xf = x.astype(jnp.float32)
    mu = jnp.mean(xf, axis=-1, keepdims=True)
    var = jnp.mean(jnp.square(xf - mu), axis=-1, keepdims=True)
    return ((xf - mu) * lax.rsqrt(var + eps) * g.astype(jnp.float32) + b.astype(jnp.float32)).astype(x.dtype)


def rms_norm(x, g, eps=1e-6):
    xf = x.astype(jnp.float32)
    return (xf * lax.rsqrt(jnp.mean(jnp.square(xf), axis=-1, keepdims=True) + eps) * g.astype(jnp.float32)).astype(x.dtype)


def split_cols(x, sizes):
    return jnp.split(x, np.cumsum(sizes)[:-1].tolist(), axis=-1)


def rope(x, pos):
    half = x.shape[-1] // 2
    inv = ROPE_BASE ** (-jnp.arange(half, dtype=jnp.float32) / half)
    ang = pos.astype(jnp.float32)[:, None] * inv[None, :]
    cos, sin = jnp.cos(ang)[:, None, :], jnp.sin(ang)[:, None, :]
    x1 = x[..., :half].astype(jnp.float32)
    x2 = x[..., half:].astype(jnp.float32)
    return jnp.concatenate([x1 * cos - x2 * sin, x1 * sin + x2 * cos], axis=-1).astype(x.dtype)


def swiglu(h, w1, w3, w2):
    return (jax.nn.silu(h @ w1) * (h @ w3)) @ w2


def modulate(x, ada_i):
    return x * (1 + ada_i[:, 1, None]) + ada_i[:, 0, None]


def to_chunks(a, chunk):
    B, T = a.shape[:2]
    return jnp.moveaxis(a.reshape(B, T // chunk, chunk, *a.shape[2:]), 1, 0)


def chunked_scalar_decay(q, k, v, log_a, s0, chunk):
    B, T, H, K = q.shape
    V = v.shape[-1]
    f32 = jnp.float32
    causal = jnp.tril(jnp.ones((chunk, chunk), bool))

    def step(s, inp):
        qi, ki, vi, ai = inp
        seg = jnp.cumsum(ai, axis=1)
        segh = jnp.moveaxis(seg, 2, 1)
        diff = segh[..., :, None] - segh[..., None, :]
        decay = jnp.exp(jnp.where(causal, diff, -jnp.inf))
        scores = jnp.einsum('bthk,bjhk->bhtj', qi, ki) * decay
        o = jnp.einsum('bhtj,bjhv->bthv', scores, vi) + jnp.einsum('bthk,bhkv->bthv', qi * jnp.exp(seg)[..., None], s)
        tail = jnp.exp(seg[:, -1:] - seg)
        s_new = jnp.exp(seg[:, -1])[:, :, None, None] * s + jnp.einsum('bjhk,bjhv->bhkv', ki * tail[..., None], vi)
        return s_new, o

    xs = tuple(to_chunks(a.astype(f32), chunk) for a in (q, k, v, log_a))
    s_fin, o = lax.scan(step, s0.astype(f32), xs)
    return jnp.moveaxis(o, 0, 1).reshape(B, T, H, V).astype(v.dtype), s_fin


def chunked_vector_decay(q, k, v, log_a, s0, chunk):
    B, T, H, K = q.shape
    V = v.shape[-1]
    f32 = jnp.float32
    causal = jnp.tril(jnp.ones((chunk, chunk), bool))[None, :, :, None, None]

    def step(s, inp):
        qi, ki, vi, ai = inp
        seg = jnp.cumsum(ai, axis=1)
        diff = seg[:, :, None] - seg[:, None, :]
        decay = jnp.exp(jnp.where(causal, diff, -jnp.inf))
        scores = jnp.einsum('bthk,bjhk,btjhk->bhtj', qi, ki, decay)
        o = jnp.einsum('bhtj,bjhv->bthv', scores, vi) + jnp.einsum('bthk,bhkv->bthv', qi * jnp.exp(seg), s)
        tail = jnp.exp(seg[:, -1:] - seg)
        s_new = jnp.exp(seg[:, -1])[..., None] * s + jnp.einsum('bjhk,bjhv->bhkv', ki * tail, vi)
        return s_new, o

    xs = tuple(to_chunks(a.astype(f32), chunk) for a in (q, k, v, log_a))
    s_fin, o = lax.scan(step, s0.astype(f32), xs)
    return jnp.moveaxis(o, 0, 1).reshape(B, T, H, V).astype(v.dtype), s_fin


def mla_prompt(q_lat, q_rope, lat, k_rope):
    B, T, H, R = q_lat.shape
    blk = math.gcd(T, Q_BLOCK)
    nb = T // blk
    scale = (MLA_NOPE + MLA_ROPE) ** -0.5
    key_pos = jnp.arange(T)

    def one_block(args):
        ql, qr, start = args
        s = (jnp.einsum('bthr,bsr->bhts', ql, lat) + jnp.einsum('bthe,bse->bhts', qr, k_rope)).astype(jnp.float32) * scale
        q_pos = start + jnp.arange(blk)
        s = jnp.where(key_pos[None, :] <= q_pos[:, None], s, -jnp.inf)
        prob = jax.nn.softmax(s, axis=-1).astype(lat.dtype)
        return jnp.einsum('bhts,bsr->bthr', prob, lat)

    o = lax.map(one_block, (to_chunks(q_lat, blk), to_chunks(q_rope, blk), jnp.arange(nb) * blk))
    return jnp.moveaxis(o, 0, 1).reshape(B, T, H, R)


def mla_sample(q_lat, q_rope, lat, k_rope, past_lat, past_rope):
    T = q_lat.shape[1]
    P = past_lat.shape[1]
    scale = (MLA_NOPE + MLA_ROPE) ** -0.5
    s_past = jnp.einsum('bthr,bsr->bhts', q_lat, past_lat) + jnp.einsum('bthe,bse->bhts', q_rope, past_rope)
    s_new = jnp.einsum('bthr,bsr->bhts', q_lat, lat) + jnp.einsum('bthe,bse->bhts', q_rope, k_rope)
    s_new = jnp.where(jnp.tril(jnp.ones((T, T), bool)), s_new.astype(jnp.float32), -jnp.inf)
    s = jnp.concatenate([s_past.astype(jnp.float32), s_new], axis=-1) * scale
    prob = jax.nn.softmax(s, axis=-1).astype(lat.dtype)
    return jnp.einsum('bhts,bsr->bthr', prob[..., :P], past_lat) + jnp.einsum('bhts,bsr->bthr', prob[..., P:], lat)


def even_mixer(h, pos, j, p, ret_s0, past_lat, past_rope):
    B, T, _ = h.shape
    c_q, c_kv, k_rope_raw, rq, rk, rv, rg = split_cols(h @ p['w_in_even'][j], EVEN_SPLITS)
    q = (rms_norm(c_q, p['mla_q_norm'][j]) @ p['w_uq'][j]).reshape(B, T, MLA_HEADS, MLA_NOPE + MLA_ROPE)
    q_nope, q_rope = q[..., :MLA_NOPE], rope(q[..., MLA_NOPE:], pos)
    lat = rms_norm(c_kv, p['mla_kv_norm'][j])
    k_rope = rope(k_rope_raw[:, :, None, :], pos)[:, :, 0]
    q_lat = jnp.einsum('bthn,rhn->bthr', q_nope, p['w_uk'][j])
    if past_lat is None:
        o_lat = mla_prompt(q_lat, q_rope, lat, k_rope)
    else:
        o_lat = mla_sample(q_lat, q_rope, lat, k_rope, past_lat, past_rope)
    mla_out = jnp.einsum('bthr,rhv->bthv', o_lat, p['w_uv'][j]).reshape(B, T, MLA_HEADS * MLA_V)
    rq = rope(rq.reshape(B, T, RET_HEADS, RET_DK), pos)
    rk = rope(rk.reshape(B, T, RET_HEADS, RET_DK), pos) * RET_DK ** -0.5
    rv = rv.reshape(B, T, RET_HEADS, RET_DV)
    log_gamma = jnp.log1p(-jnp.exp2(-5.0 - jnp.arange(RET_HEADS, dtype=jnp.float32)))
    log_a = jnp.broadcast_to(log_gamma, (B, T, RET_HEADS))
    ro, ret_s = chunked_scalar_decay(rq, rk, rv, log_a, ret_s0, math.gcd(T, RET_CHUNK))
    ret_out = jax.nn.silu(rg) * rms_norm(ro, p['ret_norm'][j]).reshape(B, T, RET_HEADS * RET_DV)
    out = jnp.concatenate([mla_out, ret_out], axis=-1) @ p['w_out_even'][j]
    return out, lat, k_rope, ret_s


def odd_mixer(h, j, p, gla_s0, ssm_s0, conv0):
    B, T, _ = h.shape
    gq, gk, gv, g_lr, gr, z, xbc, dt_raw = split_cols(h @ p['w_in_odd'][j], ODD_SPLITS)
    gq = gq.reshape(B, T, GLA_HEADS, GLA_DK) * GLA_DK ** -0.5
    gk = gk.reshape(B, T, GLA_HEADS, GLA_DK)
    gv = gv.reshape(B, T, GLA_HEADS, GLA_DV)
    gate_pre = (g_lr @ p['gla_w_gate'][j] + p['gla_b_gate'][j]).astype(jnp.float32)
    log_a = (jax.nn.log_sigmoid(gate_pre) / GLA_TAU).reshape(B, T, GLA_HEADS, GLA_DK)
    go, gla_s = chunked_vector_decay(gq, gk, gv, log_a, gla_s0, math.gcd(T, GLA_CHUNK))
    gla_out = jax.nn.silu(gr) * rms_norm(go, p['gla_norm'][j]).reshape(B, T, GLA_HEADS * GLA_DV)
    xpad = jnp.concatenate([conv0.astype(xbc.dtype), xbc], axis=1)
    cw = p['ssm_conv_w'][j]
    conv = p['ssm_conv_b'][j] + sum(xpad[:, i:i + T] * cw[i] for i in range(SSM_CONV))
    conv_state = xpad[:, T:]
    xs, bm, cm = split_cols(jax.nn.silu(conv), (SSM_INNER, SSM_GROUPS * SSM_DSTATE, SSM_GROUPS * SSM_DSTATE))
    xs = xs.reshape(B, T, SSM_HEADS, SSM_HEADDIM)
    rep = SSM_HEADS // SSM_GROUPS
    bm = jnp.repeat(bm.reshape(B, T, SSM_GROUPS, SSM_DSTATE), rep, axis=2)
    cm = jnp.repeat(cm.reshape(B, T, SSM_GROUPS, SSM_DSTATE), rep, axis=2)
    dt = jax.nn.softplus((dt_raw + p['ssm_dt_bias'][j]).astype(jnp.float32))
    a = -jnp.exp(p['ssm_a_log'][j].astype(jnp.float32))
    y, ssm_s = chunked_scalar_decay(cm, bm, xs * dt[..., None].astype(xs.dtype), dt * a, ssm_s0, math.gcd(T, SSM_CHUNK))
    y = y + p['ssm_d'][j][:, None] * xs
    y = (y * jax.nn.silu(z.reshape(B, T, SSM_HEADS, SSM_HEADDIM))).reshape(B, T, SSM_GROUPS, SSM_INNER // SSM_GROUPS)
    y = rms_norm(y, p['ssm_norm'][j].reshape(SSM_GROUPS, SSM_INNER // SSM_GROUPS)).reshape(B, T, SSM_INNER)
    out = jnp.concatenate([gla_out, y], axis=-1) @ p['w_out_odd'][j]
    return out, gla_s, ssm_s, conv_state


def run_trunk(x, c, pos, p, ret_s0, gla_s0, ssm_s0, conv0, past_lat, past_rope, page_table):
    B, T, D = x.shape
    lat_l, rope_l, ret_l, gla_l, ssm_l, conv_l = [], [], [], [], [], []
    sc = jax.nn.silu(c)
    for l in range(DEPTH):
        ada = (sc @ p['w_ada'][l] + p['b_ada'][l]).reshape(B, 3, 3, D)
        f = swiglu(modulate(x, ada[:, 0]), p['ffn_w1'][l, 0], p['ffn_w3'][l, 0], p['ffn_w2'][l, 0])
        x = layer_norm(ALPHA * x + 0.5 * ada[:, 0, 2, None] * f, p['ln_g'][l, 0], p['ln_b'][l, 0])
        h = modulate(x, ada[:, 1])
        j = l // 2
        if l % 2 == 0:
            if past_lat is None:
                pl, pr = None, None
            else:
                pl = past_lat[j][page_table].reshape(B, -1, MLA_KV_RANK)
                pr = past_rope[j][page_table].reshape(B, -1, MLA_ROPE)
            mix, lat, kr, rs = even_mixer(h, pos, j, p, ret_s0[j], pl, pr)
            lat_l.append(lat)
            rope_l.append(kr)
            ret_l.append(rs)
        else:
            mix, gs, ss, cs = odd_mixer(h, j, p, gla_s0[j], ssm_s0[j], conv0[j])
            gla_l.append(gs)
            ssm_l.append(ss)
            conv_l.append(cs)
        x = layer_norm(ALPHA * x + ada[:, 1, 2, None] * mix, p['ln_g'][l, 1], p['ln_b'][l, 1])
        f = swiglu(modulate(x, ada[:, 2]), p['ffn_w1'][l, 1], p['ffn_w3'][l, 1], p['ffn_w2'][l, 1])
        x = layer_norm(ALPHA * x + 0.5 * ada[:, 2, 2, None] * f, p['ln_g'][l, 2], p['ln_b'][l, 2])
    return x, jnp.stack(lat_l), jnp.stack(rope_l), jnp.stack(ret_l), jnp.stack(gla_l), jnp.stack(ssm_l), jnp.stack(conv_l)


def setup_inputs(seed: int = 0) -> dict:
    key = jax.random.key(seed)
    k = jax.random.split(key, 40)
    f32 = jnp.float32
    D = D_MODEL

    def nrm(i, shape, scale):
        return jax.random.normal(k[i], shape, f32) * scale

    n_pages = PAST_LEN // PAGE_SIZE
    n_used = DEC_BATCH * n_pages
    n_pool = n_used + n_used // 4
    page_table = jax.random.permutation(k[6], n_pool)[:n_used].reshape(DEC_BATCH, n_pages).astype(jnp.int32)
    dt0 = jnp.exp(jax.random.uniform(k[32], (N_ODD, SSM_HEADS), f32, math.log(1e-3), math.log(1e-1)))
    dt_bias = dt0 + jnp.log(-jnp.expm1(-dt0))
    a_log = jnp.log(jax.random.uniform(k[33], (N_ODD, SSM_HEADS), f32, 1.0, 16.0))
    return {
        'x_prompt': nrm(0, (BATCH, SEQ, D), 1.0),
        'x_sample': nrm(1, (DEC_BATCH, DEC_SEQ, D), 1.0),
        'cache_mla_latent': nrm(4, (N_EVEN, n_pool, PAGE_SIZE, MLA_KV_RANK), 1.0),
        'cache_mla_krope': nrm(5, (N_EVEN, n_pool, PAGE_SIZE, MLA_ROPE), 1.0),
        'page_table': page_table,
        'state_retention': nrm(7, (N_EVEN, DEC_BATCH, RET_HEADS, RET_DK, RET_DV), 0.5),
        'state_gla': nrm(8, (N_ODD, DEC_BATCH, GLA_HEADS, GLA_DK, GLA_DV), 0.5),
        'state_ssm': nrm(9, (N_ODD, DEC_BATCH, SSM_HEADS, SSM_DSTATE, SSM_HEADDIM), 0.5),
        'state_conv': nrm(10, (N_ODD, DEC_BATCH, SSM_CONV - 1, SSM_CONV_DIM), 1.0),
        'c_prompt': nrm(2, (BATCH, D), 1.0),
        'c_sample': nrm(3, (DEC_BATCH, D), 1.0),
        'w_ada': nrm(11, (DEPTH, D, 9 * D), D ** -0.5),
        'b_ada': nrm(12, (DEPTH, 9 * D), 0.02),
        'ln_g': 1.0 + nrm(13, (DEPTH, 3, D), 0.02),
        'ln_b': nrm(14, (DEPTH, 3, D), 0.02),
        'ffn_w1': nrm(15, (DEPTH, 2, D, D_FF), D ** -0.5),
        'ffn_w3': nrm(16, (DEPTH, 2, D, D_FF), D ** -0.5),
        'ffn_w2': nrm(17, (DEPTH, 2, D_FF, D), BETA * D_FF ** -0.5),
        'w_in_even': nrm(18, (N_EVEN, D, EVEN_IN), D ** -0.5),
        'mla_q_norm': 1.0 + nrm(19, (N_EVEN, MLA_Q_RANK), 0.02),
        'w_uq': nrm(20, (N_EVEN, MLA_Q_RANK, MLA_HEADS * (MLA_NOPE + MLA_ROPE)), MLA_Q_RANK ** -0.5),
        'mla_kv_norm': 1.0 + nrm(21, (N_EVEN, MLA_KV_RANK), 0.02),
        'w_uk': nrm(22, (N_EVEN, MLA_KV_RANK, MLA_HEADS, MLA_NOPE), MLA_KV_RANK ** -0.5),
        'w_uv': nrm(23, (N_EVEN, MLA_KV_RANK, MLA_HEADS, MLA_V), MLA_KV_RANK ** -0.5),
        'ret_norm': 1.0 + nrm(24, (N_EVEN, RET_DV), 0.02),
        'w_out_even': nrm(25, (N_EVEN, EVEN_MIX, D), BETA * EVEN_MIX ** -0.5),
        'w_in_odd': nrm(26, (N_ODD, D, ODD_IN), D ** -0.5),
        'gla_w_gate': nrm(27, (N_ODD, GLA_GATE_RANK, GLA_HEADS * GLA_DK), GLA_GATE_RANK ** -0.5),
        'gla_b_gate': nrm(28, (N_ODD, GLA_HEADS * GLA_DK), 0.1),
        'gla_norm': 1.0 + nrm(29, (N_ODD, GLA_DV), 0.02),
        'ssm_conv_w': nrm(30, (N_ODD, SSM_CONV, SSM_CONV_DIM), SSM_CONV ** -0.5),
        'ssm_conv_b': nrm(31, (N_ODD, SSM_CONV_DIM), 0.02),
        'ssm_dt_bias': dt_bias,
        'ssm_a_log': a_log,
        'ssm_d': 1.0 + nrm(34, (N_ODD, SSM_HEADS), 0.02),
        'ssm_norm': 1.0 + nrm(35, (N_ODD, SSM_INNER), 0.02),
        'w_out_odd': nrm(36, (N_ODD, ODD_MIX, D), BETA * ODD_MIX ** -0.5),
    }


def reference(x_prompt, x_sample, cache_mla_latent, cache_mla_krope, page_table, state_retention, state_gla, state_ssm, state_conv, c_prompt, c_sample, w_ada, b_ada, ln_g, ln_b, ffn_w1, ffn_w3, ffn_w2, w_in_even, mla_q_norm, w_uq, mla_kv_norm, w_uk, w_uv, ret_norm, w_out_even, w_in_odd, gla_w_gate, gla_b_gate, gla_norm, ssm_conv_w, ssm_conv_b, ssm_dt_bias, ssm_a_log, ssm_d, ssm_norm, w_out_odd):
    p = {'w_ada': w_ada, 'b_ada': b_ada, 'ln_g': ln_g, 'ln_b': ln_b,
         'ffn_w1': ffn_w1, 'ffn_w3': ffn_w3, 'ffn_w2': ffn_w2,
         'w_in_even': w_in_even, 'mla_q_norm': mla_q_norm, 'w_uq': w_uq, 'mla_kv_norm': mla_kv_norm,
         'w_uk': w_uk, 'w_uv': w_uv, 'ret_norm': ret_norm, 'w_out_even': w_out_even,
         'w_in_odd': w_in_odd, 'gla_w_gate': gla_w_gate, 'gla_b_gate': gla_b_gate, 'gla_norm': gla_norm,
         'ssm_conv_w': ssm_conv_w, 'ssm_conv_b': ssm_conv_b, 'ssm_dt_bias': ssm_dt_bias,
         'ssm_a_log': ssm_a_log, 'ssm_d': ssm_d, 'ssm_norm': ssm_norm, 'w_out_odd': w_out_odd}
    f32 = jnp.float32
    bp, tp = x_prompt.shape[:2]
    past_len = page_table.shape[1] * PAGE_SIZE
    pos_p = jnp.arange(tp, dtype=jnp.int32)
    pos_s = past_len + jnp.arange(x_sample.shape[1], dtype=jnp.int32)
    ret0 = jnp.zeros((N_EVEN, bp, RET_HEADS, RET_DK, RET_DV), f32)
    gla0 = jnp.zeros((N_ODD, bp, GLA_HEADS, GLA_DK, GLA_DV), f32)
    ssm0 = jnp.zeros((N_ODD, bp, SSM_HEADS, SSM_DSTATE, SSM_HEADDIM), f32)
    conv0 = jnp.zeros((N_ODD, bp, SSM_CONV - 1, SSM_CONV_DIM), x_prompt.dtype)
    y_prompt, lat_p, krope_p, ret_p, gla_p, ssm_p, conv_p = run_trunk(
        x_prompt, c_prompt, pos_p, p, ret0, gla0, ssm0, conv0, None, None, None)
    y_sample, lat_s, krope_s, ret_s, gla_s, ssm_s, conv_s = run_trunk(
        x_sample, c_sample, pos_s, p, state_retention, state_gla, state_ssm, state_conv,
        cache_mla_latent, cache_mla_krope, page_table)
    return (y_prompt, y_sample, lat_p, lat_s, krope_p, krope_s, ret_p, ret_s, gla_p, gla_s, ssm_p, ssm_s, conv_p, conv_s)
```

```python
import functools
import math

import jax
import jax.numpy as jnp
import numpy as np
from jax import lax
from jax.experimental import pallas as pl
from jax.experimental.pallas import tpu as pltpu

f32 = jnp.float32
bf16 = jnp.bfloat16
HIGHEST = lax.Precision.HIGHEST

DEPTH = 4
ALPHA = (2.0 * DEPTH) ** 0.25
ROPE_BASE = 10000.0
PAGE_SIZE = 128
MLA_HEADS, MLA_Q_RANK, MLA_KV_RANK, MLA_NOPE, MLA_ROPE, MLA_V = 8, 256, 128, 64, 32, 64
RET_HEADS, RET_DK, RET_DV = 4, 64, 128
GLA_HEADS, GLA_DK, GLA_DV, GLA_GATE_RANK, GLA_TAU = 4, 64, 128, 16, 16.0
SSM_HEADS, SSM_HEADDIM, SSM_GROUPS, SSM_DSTATE, SSM_CONV = 16, 64, 2, 128, 4
SSM_INNER = SSM_HEADS * SSM_HEADDIM
SSM_CONV_DIM = SSM_INNER + 2 * SSM_GROUPS * SSM_DSTATE
LANES = 128
SUBLANES = 8

FFN_ROWS = 1024
FFN_FCHUNK = 256
MIX_ROWS = 512
ATT_TQ = 256
SCAN_CHUNK = 256
GLA_CHUNK = 64
GLA_SUB = 16
PAGES_PER_STEP = 8
VMEM_LIMIT = 56 * 1024 * 1024


def _cparams(*sem):
    return pltpu.CompilerParams(dimension_semantics=sem, vmem_limit_bytes=VMEM_LIMIT)


def _row_tiles(B, T, rows):
    if T >= rows:
        return 1, rows
    return min(B, rows // T), T


def _dot(a, b, precision=None):
    return jnp.dot(a, b, preferred_element_type=f32, precision=precision)


def _dot_nt(a, b):
    return lax.dot_general(a, b, (((1,), (1,)), ((), ())), preferred_element_type=f32)


def _dot_tn(a, b, precision=None):
    return lax.dot_general(a, b, (((0,), (0,)), ((), ())), preferred_element_type=f32, precision=precision)


def _silu(x):
    return x * jax.nn.sigmoid(x)


def _softplus(x):
    return jnp.maximum(x, 0.0) + jnp.log1p(jnp.exp(-jnp.abs(x)))


def _log_sigmoid(x):
    return jnp.minimum(x, 0.0) - jnp.log1p(jnp.exp(-jnp.abs(x)))


def _rms(x, g, eps=1e-6):
    return x * lax.rsqrt(jnp.mean(x * x, axis=-1, keepdims=True) + eps) * g


def _layer_norm(y, g, b, eps=1e-5):
    mu = jnp.mean(y, axis=-1, keepdims=True)
    yc = y - mu
    var = jnp.mean(yc * yc, axis=-1, keepdims=True)
    return yc * lax.rsqrt(var + eps) * g + b


def _modulate(x_ref, ada_ref, sub):
    return x_ref[...] * (1.0 + ada_ref[:, 3 * sub + 1:3 * sub + 2, :]) + ada_ref[:, 3 * sub:3 * sub + 1, :]


def _residual_ln(x_ref, ada_ref, sub, coef, f, g_ref, b_ref):
    gate = ada_ref[:, 3 * sub + 2:3 * sub + 3, :]
    y = ALPHA * x_ref[...] + coef * gate * f.reshape(x_ref.shape)
    return _layer_norm(y, g_ref[...], b_ref[...])


def _ada_kernel(c_ref, w_ref, b_ref, o_ref):
    sc = _silu(c_ref[...]).astype(bf16)
    o_ref[...] = _dot(sc, w_ref[...].astype(bf16)) + b_ref[...]


def _ada_call(c_all, w_ada, b_ada):
    n, d = c_all.shape
    depth, _, wide = w_ada.shape
    tn = 1024
    return pl.pallas_call(
        _ada_kernel,
        out_shape=jax.ShapeDtypeStruct((depth, n, wide), f32),
        grid=(depth, wide // tn),
        in_specs=[
            pl.BlockSpec((n, d), lambda l, j: (0, 0)),
            pl.BlockSpec((None, d, tn), lambda l, j: (l, 0, j)),
            pl.BlockSpec((None, 1, tn), lambda l, j: (l, 0, j)),
        ],
        out_specs=pl.BlockSpec((None, n, tn), lambda l, j: (l, 0, j)),
        compiler_params=_cparams("parallel", "parallel"),
        name="ada_proj",
    )(c_all, w_ada, b_ada.reshape(depth, 1, wide))


def _ffn_kernel(x_ref, ada_ref, w1_ref, w3_ref, w2_ref, g_ref, b_ref, o_ref, h_sc, acc_sc, *, sub):
    j = pl.program_id(2)

    @pl.when(j == 0)
    def _():
        h = _modulate(x_ref, ada_ref, sub)
        h_sc[...] = h.reshape(h_sc.shape).astype(bf16)
        acc_sc[...] = jnp.zeros_like(acc_sc)

    h = h_sc[...]
    a = _dot(h, w1_ref[...])
    b = _dot(h, w3_ref[...])
    acc_sc[...] += _dot((_silu(a) * b).astype(bf16), w2_ref[...])

    @pl.when(j == pl.num_programs(2) - 1)
    def _():
        o_ref[...] = _residual_ln(x_ref, ada_ref, sub, 0.5, acc_sc[...], g_ref, b_ref)


def _ffn_call(x, ada, w1, w3, w2, ln_g, ln_b, layer, half):
    B, T, D = x.shape
    F = w1.shape[-1]
    bB, bT = _row_tiles(B, T, FFN_ROWS)
    sub = 2 * half
    row = lambda b, t, j: (b, t, 0)
    return pl.pallas_call(
        functools.partial(_ffn_kernel, sub=sub),
        out_shape=jax.ShapeDtypeStruct((B, T, D), f32),
        grid=(B // bB, T // bT, F // FFN_FCHUNK),
        in_specs=[
            pl.BlockSpec((bB, bT, D), row),
            pl.BlockSpec((bB, 9, D), lambda b, t, j: (b, 0, 0)),
            pl.BlockSpec((None, None, D, FFN_FCHUNK), lambda b, t, j: (layer, half, 0, j)),
            pl.BlockSpec((None, None, D, FFN_FCHUNK), lambda b, t, j: (layer, half, 0, j)),
            pl.BlockSpec((None, None, FFN_FCHUNK, D), lambda b, t, j: (layer, half, j, 0)),
            pl.BlockSpec((None, None, 1, D), lambda b, t, j: (layer, sub, 0, 0)),
            pl.BlockSpec((None, None, 1, D), lambda b, t, j: (layer, sub, 0, 0)),
        ],
        out_specs=pl.BlockSpec((bB, bT, D), row),
        scratch_shapes=[pltpu.VMEM((bB * bT, D), bf16), pltpu.VMEM((bB * bT, D), f32)],
        compiler_params=_cparams("parallel", "parallel", "arbitrary"),
        name="ffn",
    )(x, ada, w1, w3, w2, ln_g, ln_b)


def _rope_tables(pos, half, width, used):
    inv = ROPE_BASE ** (-jnp.arange(half, dtype=f32) / half)
    ang = pos.astype(f32)[:, None] * inv[None, :]
    cos, sin = jnp.cos(ang), jnp.sin(ang)
    lane = np.arange(width)
    idx = lane % half
    live = lane < used
    first = (lane % (2 * half)) < half
    c = jnp.where(live[None, :], cos[:, idx], 0.0)
    s1 = jnp.where((live & first)[None, :], -sin[:, idx], 0.0)
    s2 = jnp.where((live & ~first)[None, :], sin[:, idx], 0.0)
    return c, s1, s2


def _rope(x, c, s1, s2, half):
    w = x.shape[-1]
    return x * c + pltpu.roll(x, w - half, 1) * s1 + pltpu.roll(x, half, 1) * s2


EVEN_COLS = 2048


def _even_pre_kernel(x_ref, ada_ref, win_ref, qn_ref, wqn_ref, wqr_ref, wuk_ref, kvn_ref,
                     cm_ref, s1m_ref, s2m_ref, cr_ref, s1r_ref, s2r_ref,
                     q_ref, kc_ref, lat_ref, kr_ref, rq_ref, rk_ref, rv_ref, rg_ref):
    bB, bT, D = x_ref.shape
    R = bB * bT
    h = _modulate(x_ref, ada_ref, 1).reshape(R, D).astype(bf16)
    proj = _dot(h, win_ref[...])
    cqn = _rms(proj[:, 0:256], qn_ref[...]).astype(bf16)
    q_nope = _dot(cqn, wqn_ref[...]).astype(bf16)
    q_rope = _dot(cqn, wqr_ref[...])
    lat = _rms(proj[:, 256:384], kvn_ref[...])
    cm, s1m, s2m = cm_ref[...], s1m_ref[...], s2m_ref[...]
    half_m = MLA_ROPE // 2
    kr = _rope(proj[:, 384:512], cm, s1m, s2m, half_m)
    lat_ref[...] = lat.reshape(bB, bT, LANES)
    kr_ref[...] = kr[:, :MLA_ROPE].reshape(bB, bT, MLA_ROPE)
    kc_ref[...] = jnp.concatenate([lat, kr], axis=1).astype(bf16).reshape(bB, bT, 2 * LANES)
    for p in range(MLA_HEADS // 2):
        q_lat2 = _dot(q_nope[:, LANES * p:LANES * (p + 1)], wuk_ref[p])
        for hh in range(2):
            hd = 2 * p + hh
            qr = _rope(q_rope[:, LANES * hd:LANES * (hd + 1)], cm, s1m, s2m, half_m)
            qc = jnp.concatenate([q_lat2[:, LANES * hh:LANES * (hh + 1)], qr], axis=1)
            q_ref[:, hd] = qc.astype(bf16).reshape(bB, bT, 2 * LANES)
    cr, s1r, s2r = cr_ref[...], s1r_ref[...], s2r_ref[...]
    half_r = RET_DK // 2
    rq_ref[...] = _rope(proj[:, 512:768], cr, s1r, s2r, half_r).reshape(bB, bT, 256)
    rk_ref[...] = (_rope(proj[:, 768:1024], cr, s1r, s2r, half_r) * RET_DK ** -0.5).reshape(bB, bT, 256)
    rv_ref[...] = proj[:, 1024:1536].reshape(bB, bT, 512)
    rg_ref[...] = proj[:, 1536:2048].reshape(bB, bT, 512)


def _even_pre_call(x, ada, wts, tabs):
    B, T, D = x.shape
    bB, bT = _row_tiles(B, T, MIX_ROWS)
    R = bB * bT
    row = lambda b, t: (b, t, 0)
    full = lambda shape: pl.BlockSpec(shape, lambda b, t: (0,) * len(shape))
    tab_map = (lambda b, t: (t, 0)) if bB == 1 else (lambda b, t: (0, 0))
    tab = lambda w: pl.BlockSpec((R, w), tab_map)
    out3 = lambda w, dt: jax.ShapeDtypeStruct((B, T, w), dt)
    return pl.pallas_call(
        _even_pre_kernel,
        out_shape=(
            jax.ShapeDtypeStruct((B, MLA_HEADS, T, 256), bf16),
            out3(256, bf16), out3(MLA_KV_RANK, f32), out3(MLA_ROPE, f32),
            out3(256, f32), out3(256, f32), out3(512, f32), out3(512, f32),
        ),
        grid=(B // bB, T // bT),
        in_specs=[
            pl.BlockSpec((bB, bT, D), row),
            pl.BlockSpec((bB, 9, D), lambda b, t: (b, 0, 0)),
            full((D, EVEN_COLS)), full((1, 256)), full((256, 512)), full((256, 1024)),
            full((4, LANES, 256)), full((1, LANES)),
            tab(LANES), tab(LANES), tab(LANES), tab(256), tab(256), tab(256),
        ],
        out_specs=(
            pl.BlockSpec((bB, MLA_HEADS, bT, 256), lambda b, t: (b, 0, t, 0)),
            pl.BlockSpec((bB, bT, 256), row), pl.BlockSpec((bB, bT, MLA_KV_RANK), row),
            pl.BlockSpec((bB, bT, MLA_ROPE), row),
            pl.BlockSpec((bB, bT, 256), row), pl.BlockSpec((bB, bT, 256), row),
            pl.BlockSpec((bB, bT, 512), row), pl.BlockSpec((bB, bT, 512), row),
        ),
        compiler_params=_cparams("parallel", "parallel"),
        name="even_pre",
    )(x, ada, wts["w_in"], wts["q_norm"], wts["w_q_nope"], wts["w_q_rope"], wts["w_uk_bd"], wts["kv_norm"], *tabs)


MLA_SCALE = (MLA_NOPE + MLA_ROPE) ** -0.5


def _softmax_step(s, v, m_sc, l_sc, acc_sc):
    m_prev = m_sc[...]
    m_new = jnp.maximum(m_prev, jnp.max(s, axis=-1, keepdims=True))
    p = jnp.exp(s - m_new)
    alpha = jnp.exp(m_prev - m_new)
    l_sc[...] = alpha * l_sc[...] + jnp.sum(p, axis=-1, keepdims=True)
    acc_sc[...] = alpha * acc_sc[...] + _dot(p.astype(bf16), v)
    m_sc[...] = m_new


def _mla_prompt_kernel(q_ref, kc_ref, o_ref, m_sc, l_sc, acc_sc, *, tq):
    qi = pl.program_id(1)
    R = MLA_HEADS * tq
    q = q_ref[0].reshape(R, 2 * LANES)
    m_sc[...] = jnp.full_like(m_sc, -jnp.inf)
    l_sc[...] = jnp.zeros_like(l_sc)
    acc_sc[...] = jnp.zeros_like(acc_sc)

    def block(j, masked):
        kblk = kc_ref[0, pl.ds(pl.multiple_of(j * tq, tq), tq), :]
        s = _dot_nt(q, kblk) * MLA_SCALE
        if masked:
            qpos = lax.broadcasted_iota(jnp.int32, (R, tq), 0) % tq
            kpos = lax.broadcasted_iota(jnp.int32, (R, tq), 1)
            s = jnp.where(kpos <= qpos, s, -jnp.inf)
        _softmax_step(s, kblk[:, :LANES], m_sc, l_sc, acc_sc)

    def body(j, carry):
        block(j, False)
        return carry

    lax.fori_loop(0, qi, body, 0)
    block(qi, True)
    o = (acc_sc[...] / l_sc[...]).astype(bf16)
    for h in range(MLA_HEADS):
        o_ref[0, :, LANES * h:LANES * (h + 1)] = o[h * tq:(h + 1) * tq]


def _mla_prompt_call(q, kc):
    B, H, T, W = q.shape
    tq = min(ATT_TQ, T)
    R = H * tq
    return pl.pallas_call(
        functools.partial(_mla_prompt_kernel, tq=tq),
        out_shape=jax.ShapeDtypeStruct((B, T, H * LANES), bf16),
        grid=(B, T // tq),
        in_specs=[
            pl.BlockSpec((1, H, tq, W), lambda b, i: (b, 0, i, 0)),
            pl.BlockSpec((1, T, W), lambda b, i: (b, 0, 0)),
        ],
        out_specs=pl.BlockSpec((1, tq, H * LANES), lambda b, i: (b, i, 0)),
        scratch_shapes=[pltpu.VMEM((R, 1), f32), pltpu.VMEM((R, 1), f32), pltpu.VMEM((R, LANES), f32)],
        compiler_params=_cparams("parallel", "parallel"),
        name="mla_prompt",
    )(q, kc)


def _mla_sample_kernel(pt_ref, q_ref, kc_ref, *rest, T):
    P = PAGES_PER_STEP
    lat_refs, kr_refs = rest[:P], rest[P:2 * P]
    o_ref, m_sc, l_sc, acc_sc = rest[2 * P:]
    g = pl.program_id(1)
    R = MLA_HEADS * T
    q = q_ref[0].reshape(R, 2 * LANES)

    @pl.when(g == 0)
    def _():
        m_sc[...] = jnp.full_like(m_sc, -jnp.inf)
        l_sc[...] = jnp.zeros_like(l_sc)
        acc_sc[...] = jnp.zeros_like(acc_sc)

    lat = jnp.concatenate([r[...] for r in lat_refs], axis=0).astype(bf16)
    kr = jnp.concatenate([r[...] for r in kr_refs], axis=0).astype(bf16)
    s = (_dot_nt(q[:, :LANES], lat) + _dot_nt(q[:, LANES:LANES + MLA_ROPE], kr)) * MLA_SCALE
    _softmax_step(s, lat, m_sc, l_sc, acc_sc)

    @pl.when(g == pl.num_programs(1) - 1)
    def _():
        kc = kc_ref[0]
        s_new = _dot_nt(q, kc) * MLA_SCALE
        qpos = lax.broadcasted_iota(jnp.int32, (R, T), 0) % T
        kpos = lax.broadcasted_iota(jnp.int32, (R, T), 1)
        s_new = jnp.where(kpos <= qpos, s_new, -jnp.inf)
        _softmax_step(s_new, kc[:, :LANES], m_sc, l_sc, acc_sc)
        o = (acc_sc[...] / l_sc[...]).astype(bf16)
        for h in range(MLA_HEADS):
            o_ref[0, :, LANES * h:LANES * (h + 1)] = o[h * T:(h + 1) * T]


def _mla_sample_call(q, kc, cache_lat, cache_kr, page_table, j):
    B, H, T, W = q.shape
    n_pages = page_table.shape[1]
    P = PAGES_PER_STEP
    R = H * T
    page = lambda i: (lambda b, g, pt: (j, pt[b, g * P + i], 0, 0))
    grid_spec = pltpu.PrefetchScalarGridSpec(
        num_scalar_prefetch=1,
        grid=(B, n_pages // P),
        in_specs=[
            pl.BlockSpec((1, H, T, W), lambda b, g, pt: (b, 0, 0, 0)),
            pl.BlockSpec((1, T, W), lambda b, g, pt: (b, 0, 0)),
        ] + [pl.BlockSpec((None, None, PAGE_SIZE, MLA_KV_RANK), page(i)) for i in range(P)]
          + [pl.BlockSpec((None, None, PAGE_SIZE, MLA_ROPE), page(i)) for i in range(P)],
        out_specs=pl.BlockSpec((1, T, H * LANES), lambda b, g, pt: (b, 0, 0)),
        scratch_shapes=[pltpu.VMEM((R, 1), f32), pltpu.VMEM((R, 1), f32), pltpu.VMEM((R, LANES), f32)],
    )
    return pl.pallas_call(
        functools.partial(_mla_sample_kernel, T=T),
        out_shape=jax.ShapeDtypeStruct((B, T, H * LANES), bf16),
        grid_spec=grid_spec,
        compiler_params=_cparams("parallel", "arbitrary"),
        name="mla_sample",
    )(page_table, q, kc, *([cache_lat] * P), *([cache_kr] * P))


RET_LOG_GAMMA = tuple(float(np.log1p(-np.exp2(-5.0 - h))) for h in range(RET_HEADS))


def _ret_kernel(q_ref, k_ref, v_ref, s0_ref, o_ref, sf_ref, s_sc, *, L):
    c = pl.program_id(1)
    HK = RET_HEADS * RET_DK

    @pl.when(c == 0)
    def _():
        s_sc[...] = s0_ref[0].reshape(HK, RET_DV)

    q, k, v = q_ref[0], k_ref[0], v_ref[0]
    ti = lax.broadcasted_iota(jnp.int32, (L, L), 0)
    ji = lax.broadcasted_iota(jnp.int32, (L, L), 1)
    causal = ti >= ji
    dist = (ti - ji).astype(f32)
    tcol = lax.broadcasted_iota(jnp.int32, (L, 1), 0).astype(f32)
    lane_head = lax.broadcasted_iota(jnp.int32, (1, HK), 1) // RET_DK
    row_head = lax.broadcasted_iota(jnp.int32, (HK, 1), 0) // RET_DK
    kb = k.astype(bf16)
    s = s_sc[...]
    sb = s.astype(bf16)
    upd = jnp.zeros((HK, RET_DV), f32)
    row_decay = jnp.zeros((HK, 1), f32)
    outs = []
    for h in range(RET_HEADS):
        lg = RET_LOG_GAMMA[h]
        mh = lane_head == h
        qh = jnp.where(mh, q, 0.0)
        decay = jnp.exp(jnp.where(causal, dist * lg, -jnp.inf))
        p = (_dot_nt(qh.astype(bf16), kb) * decay).astype(bf16)
        vh = v[:, RET_DV * h:RET_DV * (h + 1)].astype(bf16)
        o = _dot(p, vh) + _dot((qh * jnp.exp((tcol + 1.0) * lg)).astype(bf16), sb)
        outs.append(o)
        kt = jnp.where(mh, k, 0.0) * jnp.exp((L - 1.0 - tcol) * lg)
        upd = upd + _dot_tn(kt.astype(bf16), vh)
        row_decay = jnp.where(row_head == h, math.exp(L * lg), row_decay)
    o_ref[0] = jnp.concatenate(outs, axis=1)
    s_new = row_decay * s + upd
    s_sc[...] = s_new

    @pl.when(c == pl.num_programs(1) - 1)
    def _():
        sf_ref[0] = s_new.reshape(RET_HEADS, RET_DK, RET_DV)


def _ret_call(rq, rk, rv, s0):
    B, T, _ = rq.shape
    L = min(SCAN_CHUNK, T)
    HK, HV = RET_HEADS * RET_DK, RET_HEADS * RET_DV
    row = lambda b, c: (b, c, 0)
    st = pl.BlockSpec((1, RET_HEADS, RET_DK, RET_DV), lambda b, c: (b, 0, 0, 0))
    return pl.pallas_call(
        functools.partial(_ret_kernel, L=L),
        out_shape=(jax.ShapeDtypeStruct((B, T, HV), f32), jax.ShapeDtypeStruct(s0.shape, f32)),
        grid=(B, T // L),
        in_specs=[pl.BlockSpec((1, L, HK), row), pl.BlockSpec((1, L, HK), row), pl.BlockSpec((1, L, HV), row), st],
        out_specs=(pl.BlockSpec((1, L, HV), row), st),
        scratch_shapes=[pltpu.VMEM((HK, RET_DV), f32)],
        compiler_params=_cparams("parallel", "arbitrary"),
        name="retention",
    )(rq, rk, rv, s0)


def _head_rms(x, g, width):
    return jnp.concatenate([_rms(x[:, i:i + width], g) for i in range(0, x.shape[1], width)], axis=1)


def _even_post_kernel(x_ref, ada_ref, ol_ref, ro_ref, rg_ref, wuv_ref, rn_ref, wo_ref, g_ref, b_ref, o_ref):
    bB, bT, D = x_ref.shape
    R = bB * bT
    mla_out = _dot(ol_ref[...].reshape(R, MLA_HEADS * LANES), wuv_ref[...])
    ret = _silu(rg_ref[...].reshape(R, 512)) * _head_rms(ro_ref[...].reshape(R, 512), rn_ref[...], RET_DV)
    cat = jnp.concatenate([mla_out, ret], axis=1).astype(bf16)
    mix = _dot(cat, wo_ref[...])
    o_ref[...] = _residual_ln(x_ref, ada_ref, 1, 1.0, mix, g_ref, b_ref)


def _even_post_call(x, ada, o_lat, ro, rg, wts, ln_g, ln_b, layer):
    B, T, D = x.shape
    bB, bT = _row_tiles(B, T, MIX_ROWS)
    row = lambda b, t: (b, t, 0)
    full = lambda shape: pl.BlockSpec(shape, lambda b, t: (0,) * len(shape))
    ln = pl.BlockSpec((None, None, 1, D), lambda b, t: (layer, 1, 0, 0))
    return pl.pallas_call(
        _even_post_kernel,
        out_shape=jax.ShapeDtypeStruct((B, T, D), f32),
        grid=(B // bB, T // bT),
        in_specs=[
            pl.BlockSpec((bB, bT, D), row), pl.BlockSpec((bB, 9, D), lambda b, t: (b, 0, 0)),
            pl.BlockSpec((bB, bT, MLA_HEADS * LANES), row), pl.BlockSpec((bB, bT, 512), row),
            pl.BlockSpec((bB, bT, 512), row),
            full((MLA_HEADS * LANES, MLA_HEADS * MLA_V)), full((1, RET_DV)), full((1024, D)), ln, ln,
        ],
        out_specs=pl.BlockSpec((bB, bT, D), row),
        compiler_params=_cparams("parallel", "parallel"),
        name="even_post",
    )(x, ada, o_lat, ro, rg, wts["w_uv_bd"], wts["ret_norm"], wts["w_out"], ln_g, ln_b)


ODD_COLS = 4352
ODD_OFF = dict(gq=0, gk=256, gv=512, glr=1024, gr=1152, z=1664, xbc=2688, dt=4224)


def _odd_pre_kernel(x_ref, ada_ref, win_ref, wg_ref, bg_ref, cw_ref, cb_ref, dtb_ref, c0_ref,
                    gq_ref, gk_ref, gv_ref, la_ref, gr_ref, z_ref, xs_ref, bm_ref, cm_ref, dt_ref, cs_ref, buf):
    bB, bT, D = x_ref.shape
    R = bB * bT
    t = pl.program_id(1)
    O = ODD_OFF
    h = _modulate(x_ref, ada_ref, 1).reshape(R, D).astype(bf16)
    proj = _dot(h, win_ref[...])
    sl = lambda name, w: proj[:, O[name]:O[name] + w]
    gq_ref[...] = (sl("gq", 256) * GLA_DK ** -0.5).reshape(bB, bT, 256)
    gk_ref[...] = sl("gk", 256).reshape(bB, bT, 256)
    gv_ref[...] = sl("gv", 512).reshape(bB, bT, 512)
    gr_ref[...] = sl("gr", 512).reshape(bB, bT, 512)
    z_ref[...] = sl("z", 1024).reshape(bB, bT, 1024)
    gate_pre = _dot(sl("glr", LANES).astype(bf16), wg_ref[...]) + bg_ref[...]
    la_ref[...] = (_log_sigmoid(gate_pre) / GLA_TAU).reshape(bB, bT, 256)
    dt_ref[...] = _softplus(sl("dt", SSM_HEADS) + dtb_ref[...]).reshape(bB, bT, SSM_HEADS)

    W = SSM_CONV - 1
    lo = SUBLANES - W

    @pl.when(t == 0)
    def _():
        buf[:, lo:SUBLANES, :] = c0_ref[...]

    xbc = sl("xbc", SSM_CONV_DIM).reshape(bB, bT, SSM_CONV_DIM)
    buf[:, SUBLANES:SUBLANES + bT, :] = xbc
    cw = cw_ref[...]
    conv = cb_ref[...] + sum(buf[:, lo + i:lo + i + bT, :] * cw[i:i + 1, :] for i in range(SSM_CONV))
    tail = buf[:, bT + lo:bT + SUBLANES, :]
    buf[:, lo:SUBLANES, :] = tail

    @pl.when(t == pl.num_programs(1) - 1)
    def _():
        cs_ref[...] = tail

    act = _silu(conv)
    xs_ref[...] = act[:, :, :SSM_INNER]
    bm_ref[...] = act[:, :, SSM_INNER:SSM_INNER + 256]
    cm_ref[...] = act[:, :, SSM_INNER + 256:]


def _odd_pre_call(x, ada, wts, conv0):
    B, T, D = x.shape
    bB, bT = _row_tiles(B, T, MIX_ROWS)
    row = lambda b, t: (b, t, 0)
    full = lambda shape: pl.BlockSpec(shape, lambda b, t: (0,) * len(shape))
    out3 = lambda w: jax.ShapeDtypeStruct((B, T, w), f32)
    blk = lambda w: pl.BlockSpec((bB, bT, w), row)
    widths = (256, 256, 512, 256, 512, 1024, SSM_INNER, 256, 256, SSM_HEADS)
    cs = pl.BlockSpec((bB, SSM_CONV - 1, SSM_CONV_DIM), lambda b, t: (b, 0, 0))
    return pl.pallas_call(
        _odd_pre_kernel,
        out_shape=tuple(out3(w) for w in widths) + (jax.ShapeDtypeStruct(conv0.shape, f32),),
        grid=(B // bB, T // bT),
        in_specs=[
            blk(D), pl.BlockSpec((bB, 9, D), lambda b, t: (b, 0, 0)),
            full((D, ODD_COLS)), full((LANES, 256)), full((1, 256)),
            full((SSM_CONV, SSM_CONV_DIM)), full((1, SSM_CONV_DIM)), full((1, SSM_HEADS)), cs,
        ],
        out_specs=tuple(blk(w) for w in widths) + (cs,),
        scratch_shapes=[pltpu.VMEM((bB, SUBLANES + bT, SSM_CONV_DIM), f32)],
        compiler_params=_cparams("parallel", "arbitrary"),
        name="odd_pre",
    )(x, ada, wts["w_in"], wts["w_gate"], wts["b_gate"], wts["conv_w"], wts["conv_b"], wts["dt_bias"], conv0)


def _gla_kernel(q_ref, k_ref, la_ref, v_ref, s0_ref, sel_ref, o_ref, sf_ref, s_sc, *, L, sb):
    c = pl.program_id(1)
    HK, HV = GLA_HEADS * GLA_DK, GLA_HEADS * GLA_DV
    row_head = lax.broadcasted_iota(jnp.int32, (HK, HV), 0) // GLA_DK
    col_head = lax.broadcasted_iota(jnp.int32, (HK, HV), 1) // GLA_DV
    lane_hk = lax.broadcasted_iota(jnp.int32, (1, HK), 1) // GLA_DK
    lane_hv = lax.broadcasted_iota(jnp.int32, (1, HV), 1) // GLA_DV

    @pl.when(c == 0)
    def _():
        s_sc[...] = jnp.zeros_like(s_sc)
        for h in range(GLA_HEADS):
            s_sc[GLA_DK * h:GLA_DK * (h + 1), GLA_DV * h:GLA_DV * (h + 1)] = s0_ref[0, h]

    q, k, la, v = q_ref[0], k_ref[0], la_ref[0], v_ref[0]
    vb = v.astype(bf16)
    ti = lax.broadcasted_iota(jnp.int32, (L, L), 0)
    ji = lax.broadcasted_iota(jnp.int32, (L, L), 1)
    seg = _dot((ti >= ji).astype(f32), la, precision=HIGHEST)
    s = s_sc[...]
    o_inter = _dot((q * jnp.exp(seg)).astype(bf16), s.astype(bf16))
    t3 = lax.broadcasted_iota(jnp.int32, (sb, sb, 1), 0)
    j3 = lax.broadcasted_iota(jnp.int32, (sb, sb, 1), 1)
    for i in range(L // sb):
        r0 = i * sb
        qi, ki, si, vi = q[r0:r0 + sb], k[r0:r0 + sb], seg[r0:r0 + sb], v[r0:r0 + sb]
        diff = si[:, None, :] - si[None, :, :]
        e = qi[:, None, :] * ki[None, :, :] * jnp.exp(jnp.where(t3 >= j3, diff, -jnp.inf))
        sc = _dot(e.reshape(sb * sb, HK).astype(bf16), sel_ref[...])
        o = o_inter[r0:r0 + sb] + jnp.sum(sc.reshape(sb, sb, HV) * vi[None, :, :], axis=1)
        if i > 0:
            segp = seg[r0 - 1:r0]
            qt = qi * jnp.exp(si - segp)
            kt = k[:r0] * jnp.exp(segp - seg[:r0])
            qs = jnp.concatenate([jnp.where(lane_hk == h, qt, 0.0) for h in range(GLA_HEADS)], axis=0)
            pr = _dot_nt(qs.astype(bf16), kt.astype(bf16)).astype(bf16)
            full = _dot(pr, vb[:r0])
            for h in range(GLA_HEADS):
                o = o + jnp.where(lane_hv == h, full[h * sb:(h + 1) * sb], 0.0)
        o_ref[0, r0:r0 + sb, :] = o
    seg_last = seg[L - 1:L]
    kt = (k * jnp.exp(seg_last - seg)).astype(bf16)
    upd = _dot_tn(kt, vb)
    tot = _dot_tn(la, jnp.ones((L, GLA_DV), f32), precision=HIGHEST)
    row_decay = jnp.concatenate([jnp.exp(tot)] * GLA_HEADS, axis=1)
    s_new = jnp.where(row_head == col_head, row_decay * s + upd, 0.0)
    s_sc[...] = s_new

    @pl.when(c == pl.num_programs(1) - 1)
    def _():
        for h in range(GLA_HEADS):
            sf_ref[0, h] = s_new[GLA_DK * h:GLA_DK * (h + 1), GLA_DV * h:GLA_DV * (h + 1)]


def _gla_call(gq, gk, la, gv, s0, sel):
    B, T, _ = gq.shape
    L = min(GLA_CHUNK, T)
    sb = min(GLA_SUB, L)
    HK, HV = GLA_HEADS * GLA_DK, GLA_HEADS * GLA_DV
    row = lambda b, c: (b, c, 0)
    st = pl.BlockSpec((1, GLA_HEADS, GLA_DK, GLA_DV), lambda b, c: (b, 0, 0, 0))
    return pl.pallas_call(
        functools.partial(_gla_kernel, L=L, sb=sb),
        out_shape=(jax.ShapeDtypeStruct((B, T, HV), f32), jax.ShapeDtypeStruct(s0.shape, f32)),
        grid=(B, T // L),
        in_specs=[pl.BlockSpec((1, L, HK), row)] * 3 + [pl.BlockSpec((1, L, HV), row), st,
                  pl.BlockSpec((HK, HV), lambda b, c: (0, 0))],
        out_specs=(pl.BlockSpec((1, L, HV), row), st),
        scratch_shapes=[pltpu.VMEM((HK, HV), f32)],
        compiler_params=_cparams("parallel", "arbitrary"),
        name="gla",
    )(gq, gk, la, gv, s0, sel)


def _ssd_kernel(xs_ref, bm_ref, cm_ref, dt_ref, a_ref, d_ref, s0_ref, y_ref, sf_ref, s_sc, *, L):
    c = pl.program_id(1)
    P, N = SSM_HEADDIM, SSM_DSTATE
    pairs = SSM_HEADS // 2

    @pl.when(c == 0)
    def _():
        for i in range(pairs):
            s_sc[i] = jnp.concatenate([s0_ref[0, 2 * i], s0_ref[0, 2 * i + 1]], axis=1)

    dt = dt_ref[0]
    la = dt * a_ref[...]
    ti = lax.broadcasted_iota(jnp.int32, (L, L), 0)
    ji = lax.broadcasted_iota(jnp.int32, (L, L), 1)
    causal = ti >= ji
    seg = _dot(causal.astype(f32), la, precision=HIGHEST)
    seg_t = _dot_tn(la, (ti <= ji).astype(f32), precision=HIGHEST)
    seg_last = seg[L - 1:L]
    lo = lax.broadcasted_iota(jnp.int32, (1, 2 * P), 1) < P
    for g in range(SSM_GROUPS):
        cg = cm_ref[0, :, N * g:N * (g + 1)].astype(bf16)
        bg = bm_ref[0, :, N * g:N * (g + 1)].astype(bf16)
        gmat = _dot_nt(cg, bg)
        for i in range(g * pairs // SSM_GROUPS, (g + 1) * pairs // SSM_GROUPS):
            ha, hb = 2 * i, 2 * i + 1
            xp = xs_ref[0, :, 2 * P * i:2 * P * (i + 1)]
            xdt = xp * jnp.where(lo, dt[:, ha:ha + 1], dt[:, hb:hb + 1])
            xdt_b = xdt.astype(bf16)
            pa = (gmat * jnp.exp(jnp.where(causal, seg[:, ha:ha + 1] - seg_t[ha:ha + 1, :], -jnp.inf))).astype(bf16)
            pb = (gmat * jnp.exp(jnp.where(causal, seg[:, hb:hb + 1] - seg_t[hb:hb + 1, :], -jnp.inf))).astype(bf16)
            y = _dot(pa, jnp.where(lo, xdt_b, jnp.zeros_like(xdt_b))) + _dot(pb, jnp.where(lo, jnp.zeros_like(xdt_b), xdt_b))
            sp = s_sc[i]
            y = y + _dot(cg, sp.astype(bf16)) * jnp.where(lo, jnp.exp(seg[:, ha:ha + 1]), jnp.exp(seg[:, hb:hb + 1]))
            y_ref[0, :, 2 * P * i:2 * P * (i + 1)] = y + d_ref[:, 2 * P * i:2 * P * (i + 1)] * xp
            tail = jnp.where(lo, jnp.exp(seg_last[:, ha:ha + 1] - seg[:, ha:ha + 1]),
                             jnp.exp(seg_last[:, hb:hb + 1] - seg[:, hb:hb + 1]))
            chunk_decay = jnp.where(lo, jnp.exp(seg_last[:, ha:ha + 1]), jnp.exp(seg_last[:, hb:hb + 1]))
            sp_new = chunk_decay * sp + _dot_tn(bg, (xdt * tail).astype(bf16))
            s_sc[i] = sp_new

            @pl.when(c == pl.num_programs(1) - 1)
            def _():
                sf_ref[0, ha] = sp_new[:, :P]
                sf_ref[0, hb] = sp_new[:, P:]


def _ssd_call(xs, bm, cm, dt, a_row, d_row, s0):
    B, T, _ = xs.shape
    L = min(SCAN_CHUNK, T)
    row = lambda b, c: (b, c, 0)
    st = pl.BlockSpec((1, SSM_HEADS, SSM_DSTATE, SSM_HEADDIM), lambda b, c: (b, 0, 0, 0))
    return pl.pallas_call(
        functools.partial(_ssd_kernel, L=L),
        out_shape=(jax.ShapeDtypeStruct((B, T, SSM_INNER), f32), jax.ShapeDtypeStruct(s0.shape, f32)),
        grid=(B, T // L),
        in_specs=[
            pl.BlockSpec((1, L, SSM_INNER), row), pl.BlockSpec((1, L, 256), row), pl.BlockSpec((1, L, 256), row),
            pl.BlockSpec((1, L, SSM_HEADS), row),
            pl.BlockSpec((1, SSM_HEADS), lambda b, c: (0, 0)), pl.BlockSpec((1, SSM_INNER), lambda b, c: (0, 0)), st,
        ],
        out_specs=(pl.BlockSpec((1, L, SSM_INNER), row), st),
        scratch_shapes=[pltpu.VMEM((SSM_HEADS // 2, SSM_DSTATE, 2 * SSM_HEADDIM), f32)],
        compiler_params=_cparams("parallel", "arbitrary"),
        name="ssd",
    )(xs, bm, cm, dt, a_row, d_row, s0)


def _odd_post_kernel(x_ref, ada_ref, go_ref, gr_ref, y_ref, z_ref, gn_ref, sn_ref, wo_ref, g_ref, b_ref, o_ref):
    bB, bT, D = x_ref.shape
    R = bB * bT
    gla = _silu(gr_ref[...].reshape(R, 512)) * _head_rms(go_ref[...].reshape(R, 512), gn_ref[...], GLA_DV)
    yz = y_ref[...].reshape(R, SSM_INNER) * _silu(z_ref[...].reshape(R, SSM_INNER))
    gw = SSM_INNER // SSM_GROUPS
    sn = sn_ref[...]
    yn = jnp.concatenate([_rms(yz[:, gw * g:gw * (g + 1)], sn[:, gw * g:gw * (g + 1)]) for g in range(SSM_GROUPS)], axis=1)
    cat = jnp.concatenate([gla, yn], axis=1).astype(bf16)
    mix = _dot(cat, wo_ref[...])
    o_ref[...] = _residual_ln(x_ref, ada_ref, 1, 1.0, mix, g_ref, b_ref)


def _odd_post_call(x, ada, go, gr, y, z, wts, ln_g, ln_b, layer):
    B, T, D = x.shape
    bB, bT = _row_tiles(B, T, MIX_ROWS)
    row = lambda b, t: (b, t, 0)
    full = lambda shape: pl.BlockSpec(shape, lambda b, t: (0,) * len(shape))
    blk = lambda w: pl.BlockSpec((bB, bT, w), row)
    ln = pl.BlockSpec((None, None, 1, D), lambda b, t: (layer, 1, 0, 0))
    return pl.pallas_call(
        _odd_post_kernel,
        out_shape=jax.ShapeDtypeStruct((B, T, D), f32),
        grid=(B // bB, T // bT),
        in_specs=[
            blk(D), pl.BlockSpec((bB, 9, D), lambda b, t: (b, 0, 0)),
            blk(512), blk(512), blk(SSM_INNER), blk(SSM_INNER),
            full((1, GLA_DV)), full((1, SSM_INNER)), full((512 + SSM_INNER, D)), ln, ln,
        ],
        out_specs=blk(D),
        compiler_params=_cparams("parallel", "parallel"),
        name="odd_post",
    )(x, ada, go, gr, y, z, wts["gla_norm"], wts["ssm_norm"], wts["w_out"], ln_g, ln_b)


def _pad_cols(w, width):
    return jnp.pad(w, ((0, 0), (0, width - w.shape[1])))


def _prep_even(j, w_in_even, mla_q_norm, w_uq, mla_kv_norm, w_uk, w_uv, ret_norm, w_out_even):
    w = w_in_even[j]
    k0 = MLA_Q_RANK + MLA_KV_RANK
    w_in = jnp.concatenate([w[:, :k0], _pad_cols(w[:, k0:k0 + MLA_ROPE], LANES), w[:, k0 + MLA_ROPE:]], axis=1)
    uq = w_uq[j].reshape(MLA_Q_RANK, MLA_HEADS, MLA_NOPE + MLA_ROPE)
    w_q_nope = uq[:, :, :MLA_NOPE].reshape(MLA_Q_RANK, MLA_HEADS * MLA_NOPE)
    w_q_rope = jnp.pad(uq[:, :, MLA_NOPE:], ((0, 0), (0, 0), (0, LANES - MLA_ROPE))).reshape(MLA_Q_RANK, MLA_HEADS * LANES)
    eye2 = jnp.eye(2, dtype=f32)
    uk = jnp.transpose(w_uk[j], (1, 2, 0)).reshape(MLA_HEADS // 2, 2, MLA_NOPE, MLA_KV_RANK)
    w_uk_bd = (uk[:, :, :, None, :] * eye2[None, :, None, :, None]).reshape(MLA_HEADS // 2, 2 * MLA_NOPE, 2 * MLA_KV_RANK)
    eye8 = jnp.eye(MLA_HEADS, dtype=f32)
    uv = jnp.transpose(w_uv[j], (1, 0, 2))
    w_uv_bd = (uv[:, :, None, :] * eye8[:, None, :, None]).reshape(MLA_HEADS * MLA_KV_RANK, MLA_HEADS * MLA_V)
    return dict(
        w_in=w_in.astype(bf16), q_norm=mla_q_norm[j][None, :], w_q_nope=w_q_nope.astype(bf16),
        w_q_rope=w_q_rope.astype(bf16), w_uk_bd=w_uk_bd.astype(bf16), kv_norm=mla_kv_norm[j][None, :],
        w_uv_bd=w_uv_bd.astype(bf16), ret_norm=ret_norm[j][None, :], w_out=w_out_even[j].astype(bf16),
    )


def _prep_odd(j, w_in_odd, gla_w_gate, gla_b_gate, gla_norm, ssm_conv_w, ssm_conv_b, ssm_dt_bias, ssm_a_log, ssm_d,
              ssm_norm, w_out_odd):
    w = w_in_odd[j]
    c_glr = 2 * GLA_HEADS * GLA_DK + GLA_HEADS * GLA_DV
    c_dt = w.shape[1] - SSM_HEADS
    w_in = jnp.concatenate([
        w[:, :c_glr], _pad_cols(w[:, c_glr:c_glr + GLA_GATE_RANK], LANES), w[:, c_glr + GLA_GATE_RANK:c_dt],
        _pad_cols(w[:, c_dt:], LANES)], axis=1)
    w_gate = jnp.pad(gla_w_gate[j], ((0, LANES - GLA_GATE_RANK), (0, 0)))
    return dict(
        w_in=w_in.astype(bf16), w_gate=w_gate.astype(bf16), b_gate=gla_b_gate[j][None, :],
        conv_w=ssm_conv_w[j], conv_b=ssm_conv_b[j][None, :], dt_bias=ssm_dt_bias[j][None, :],
        a_row=-jnp.exp(ssm_a_log[j].astype(f32))[None, :], d_row=jnp.repeat(ssm_d[j], SSM_HEADDIM)[None, :],
        gla_norm=gla_norm[j][None, :], ssm_norm=ssm_norm[j][None, :], w_out=w_out_odd[j].astype(bf16),
    )


def _gla_select():
    r = np.arange(GLA_HEADS * GLA_DK)[:, None] // GLA_DK
    c = np.arange(GLA_HEADS * GLA_DV)[None, :] // GLA_DV
    return jnp.asarray(r == c, dtype=bf16)


def _trunk(x, ada_all, pos, p, ret_s0, gla_s0, ssm_s0, conv0, cache_lat, cache_kr, page_table):
    B, T, D = x.shape
    bB, _ = _row_tiles(B, T, MIX_ROWS)
    tabs_m = _rope_tables(pos, MLA_ROPE // 2, LANES, MLA_ROPE)
    tabs_r = _rope_tables(pos, RET_DK // 2, RET_HEADS * RET_DK, RET_HEADS * RET_DK)
    if bB > 1:
        tabs_m = tuple(jnp.tile(a, (bB, 1)) for a in tabs_m)
        tabs_r = tuple(jnp.tile(a, (bB, 1)) for a in tabs_r)
    tabs = tabs_m + tabs_r
    sel = _gla_select()
    lat_l, kr_l, ret_l, gla_l, ssm_l, conv_l = [], [], [], [], [], []
    for l in range(DEPTH):
        ada = ada_all[l]
        j = l // 2
        x = _ffn_call(x, ada, p["ffn_w1"], p["ffn_w3"], p["ffn_w2"], p["ln_g"], p["ln_b"], l, 0)
        if l % 2 == 0:
            wts = p["even"][j]
            q, kc, lat, kr, rq, rk, rv, rg = _even_pre_call(x, ada, wts, tabs)
            if page_table is None:
                o_lat = _mla_prompt_call(q, kc)
            else:
                o_lat = _mla_sample_call(q, kc, cache_lat, cache_kr, page_table, j)
            ro, rs = _ret_call(rq, rk, rv, ret_s0[j])
            x = _even_post_call(x, ada, o_lat, ro, rg, wts, p["ln_g"], p["ln_b"], l)
            lat_l.append(lat)
            kr_l.append(kr)
            ret_l.append(rs)
        else:
            wts = p["odd"][j]
            gq, gk, gv, la, gr, z, xs, bm, cm, dt, cs = _odd_pre_call(x, ada, wts, conv0[j])
            go, gs = _gla_call(gq, gk, la, gv, gla_s0[j], sel)
            y, ss = _ssd_call(xs, bm, cm, dt, wts["a_row"], wts["d_row"], ssm_s0[j])
            x = _odd_post_call(x, ada, go, gr, y, z, wts, p["ln_g"], p["ln_b"], l)
            gla_l.append(gs)
            ssm_l.append(ss)
            conv_l.append(cs)
        x = _ffn_call(x, ada, p["ffn_w1"], p["ffn_w3"], p["ffn_w2"], p["ln_g"], p["ln_b"], l, 1)
    return x, jnp.stack(lat_l), jnp.stack(kr_l), jnp.stack(ret_l), jnp.stack(gla_l), jnp.stack(ssm_l), jnp.stack(conv_l)


def kernel(x_prompt, x_sample, cache_mla_latent, cache_mla_krope, page_table, state_retention, state_gla, state_ssm, state_conv, c_prompt, c_sample, w_ada, b_ada, ln_g, ln_b, ffn_w1, ffn_w3, ffn_w2, w_in_even, mla_q_norm, w_uq, mla_kv_norm, w_uk, w_uv, ret_norm, w_out_even, w_in_odd, gla_w_gate, gla_b_gate, gla_norm, ssm_conv_w, ssm_conv_b, ssm_dt_bias, ssm_a_log, ssm_d, ssm_norm, w_out_odd):
    n_even, n_odd = (DEPTH + 1) // 2, DEPTH // 2
    bp, tp, D = x_prompt.shape
    bs, ts, _ = x_sample.shape
    p = dict(
        ffn_w1=ffn_w1.astype(bf16), ffn_w3=ffn_w3.astype(bf16), ffn_w2=ffn_w2.astype(bf16),
        ln_g=ln_g.reshape(DEPTH, 3, 1, D), ln_b=ln_b.reshape(DEPTH, 3, 1, D),
        even=[_prep_even(j, w_in_even, mla_q_norm, w_uq, mla_kv_norm, w_uk, w_uv, ret_norm, w_out_even)
              for j in range(n_even)],
        odd=[_prep_odd(j, w_in_odd, gla_w_gate, gla_b_gate, gla_norm, ssm_conv_w, ssm_conv_b, ssm_dt_bias,
                       ssm_a_log, ssm_d, ssm_norm, w_out_odd) for j in range(n_odd)],
    )
    ada_all = _ada_call(jnp.concatenate([c_prompt, c_sample], axis=0), w_ada, b_ada)
    ada_p = ada_all[:, :bp].reshape(DEPTH, bp, 9, D)
    ada_s = ada_all[:, bp:].reshape(DEPTH, bs, 9, D)
    past_len = page_table.shape[1] * PAGE_SIZE
    pos_p = jnp.arange(tp, dtype=jnp.int32)
    pos_s = past_len + jnp.arange(ts, dtype=jnp.int32)
    ret0 = jnp.zeros((n_even, bp) + state_retention.shape[2:], f32)
    gla0 = jnp.zeros((n_odd, bp) + state_gla.shape[2:], f32)
    ssm0 = jnp.zeros((n_odd, bp) + state_ssm.shape[2:], f32)
    conv0 = jnp.zeros((n_odd, bp) + state_conv.shape[2:], f32)
    out_p = _trunk(x_prompt, ada_p, pos_p, p, ret0, gla0, ssm0, conv0, None, None, None)
    out_s = _trunk(x_sample, ada_s, pos_s, p, state_retention, state_gla, state_ssm, state_conv,
                   cache_mla_latent, cache_mla_krope, page_table)
    return tuple(a for pair in zip(out_p, out_s) for a in pair)
```

```python
import functools
import math

import jax
import jax.numpy as jnp
import numpy as np
from jax import lax
from jax.experimental import pallas as pl
from jax.experimental.pallas import tpu as pltpu

f32 = jnp.float32
bf16 = jnp.bfloat16
HIGHEST = lax.Precision.HIGHEST

DEPTH = 4
ALPHA = (2.0 * DEPTH) ** 0.25
ROPE_BASE = 10000.0
PAGE_SIZE = 128
MLA_HEADS, MLA_Q_RANK, MLA_KV_RANK, MLA_NOPE, MLA_ROPE, MLA_V = 8, 256, 128, 64, 32, 64
RET_HEADS, RET_DK, RET_DV = 4, 64, 128
GLA_HEADS, GLA_DK, GLA_DV, GLA_GATE_RANK, GLA_TAU = 4, 64, 128, 16, 16.0
SSM_HEADS, SSM_HEADDIM, SSM_GROUPS, SSM_DSTATE, SSM_CONV = 16, 64, 2, 128, 4
SSM_INNER = SSM_HEADS * SSM_HEADDIM
SSM_CONV_DIM = SSM_INNER + 2 * SSM_GROUPS * SSM_DSTATE
LANES = 128
SUBLANES = 8

FFN_ROWS = 512
MIX_ROWS = 512
ATT_TQ = 512
SCAN_CHUNK = 256
GLA_CHUNK = 64
GLA_SUB = 16
VMEM_LIMIT = 56 * 1024 * 1024


def _cparams(*sem):
    return pltpu.CompilerParams(dimension_semantics=sem, vmem_limit_bytes=VMEM_LIMIT)


def _row_tiles(B, T, rows):
    if T >= rows:
        return 1, rows
    return min(B, rows // T), T


def _dot(a, b, precision=None):
    return jnp.dot(a, b, preferred_element_type=f32, precision=precision)


def _dot_nt(a, b):
    return lax.dot_general(a, b, (((1,), (1,)), ((), ())), preferred_element_type=f32)


def _dot_tn(a, b, precision=None):
    return lax.dot_general(a, b, (((0,), (0,)), ((), ())), preferred_element_type=f32, precision=precision)


def _silu(x):
    return x * jax.nn.sigmoid(x)


def _softplus(x):
    return jnp.maximum(x, 0.0) + jnp.log1p(jnp.exp(-jnp.abs(x)))


def _log_sigmoid(x):
    return jnp.minimum(x, 0.0) - jnp.log1p(jnp.exp(-jnp.abs(x)))


def _rms(x, g, eps=1e-6):
    return x * lax.rsqrt(jnp.mean(x * x, axis=-1, keepdims=True) + eps) * g


def _layer_norm(y, g, b, eps=1e-5):
    mu = jnp.mean(y, axis=-1, keepdims=True)
    yc = y - mu
    var = jnp.mean(yc * yc, axis=-1, keepdims=True)
    return yc * lax.rsqrt(var + eps) * g + b


def _modulate(x_ref, ada_ref, sub):
    return x_ref[...] * (1.0 + ada_ref[:, 3 * sub + 1:3 * sub + 2, :]) + ada_ref[:, 3 * sub:3 * sub + 1, :]


def _residual_ln(x_ref, ada_ref, sub, coef, f, g_ref, b_ref):
    gate = ada_ref[:, 3 * sub + 2:3 * sub + 3, :]
    y = ALPHA * x_ref[...] + coef * gate * f.reshape(x_ref.shape)
    return _layer_norm(y, g_ref[...], b_ref[...])


def _ada_kernel(c_ref, w_ref, b_ref, o_ref):
    sc = _silu(c_ref[...]).astype(bf16)
    o_ref[...] = _dot(sc, w_ref[...].astype(bf16)) + b_ref[...]


def _ada_call(c_all, w_ada, b_ada):
    n, d = c_all.shape
    depth, _, wide = w_ada.shape
    tn = 1024
    return pl.pallas_call(
        _ada_kernel,
        out_shape=jax.ShapeDtypeStruct((depth, n, wide), f32),
        grid=(depth, wide // tn),
        in_specs=[
            pl.BlockSpec((n, d), lambda l, j: (0, 0)),
            pl.BlockSpec((None, d, tn), lambda l, j: (l, 0, j)),
            pl.BlockSpec((None, 1, tn), lambda l, j: (l, 0, j)),
        ],
        out_specs=pl.BlockSpec((None, n, tn), lambda l, j: (l, 0, j)),
        compiler_params=_cparams("parallel", "parallel"),
        name="ada_proj",
    )(c_all, w_ada, b_ada.reshape(depth, 1, wide))


def _ffn_kernel(x_ref, ada_ref, w1_ref, w3_ref, w2_ref, g_ref, b_ref, o_ref, *, sub):
    bB, bT, D = x_ref.shape
    h = _modulate(x_ref, ada_ref, sub).reshape(bB * bT, D).astype(bf16)
    a = _dot(h, w1_ref[...])
    b = _dot(h, w3_ref[...])
    f = _dot((_silu(a) * b).astype(bf16), w2_ref[...])
    o_ref[...] = _residual_ln(x_ref, ada_ref, sub, 0.5, f, g_ref, b_ref)


def _ffn_call(x, ada, w1, w3, w2, ln_g, ln_b, layer, half):
    B, T, D = x.shape
    F = w1.shape[-1]
    bB, bT = _row_tiles(B, T, FFN_ROWS)
    sub = 2 * half
    row = lambda b, t: (b, t, 0)
    resident = lambda shape: pl.BlockSpec(shape, lambda b, t: (layer, half, 0, 0), pipeline_mode=pl.Buffered(1))
    ln = pl.BlockSpec((None, None, 1, D), lambda b, t: (layer, sub, 0, 0))
    return pl.pallas_call(
        functools.partial(_ffn_kernel, sub=sub),
        out_shape=jax.ShapeDtypeStruct((B, T, D), f32),
        grid=(B // bB, T // bT),
        in_specs=[
            pl.BlockSpec((bB, bT, D), row),
            pl.BlockSpec((bB, 9, D), lambda b, t: (b, 0, 0)),
            resident((None, None, D, F)), resident((None, None, D, F)), resident((None, None, F, D)), ln, ln,
        ],
        out_specs=pl.BlockSpec((bB, bT, D), row),
        compiler_params=_cparams("parallel", "parallel"),
        name="ffn",
    )(x, ada, w1, w3, w2, ln_g, ln_b)


def _rope_tables(pos, half, width, used):
    inv = ROPE_BASE ** (-jnp.arange(half, dtype=f32) / half)
    ang = pos.astype(f32)[:, None] * inv[None, :]
    cos, sin = jnp.cos(ang), jnp.sin(ang)
    lane = np.arange(width)
    idx = lane % half
    live = lane < used
    first = (lane % (2 * half)) < half
    c = jnp.where(live[None, :], cos[:, idx], 0.0)
    s1 = jnp.where((live & first)[None, :], -sin[:, idx], 0.0)
    s2 = jnp.where((live & ~first)[None, :], sin[:, idx], 0.0)
    return c, s1, s2


def _rope(x, c, s1, s2, half):
    w = x.shape[-1]
    return x * c + pltpu.roll(x, w - half, 1) * s1 + pltpu.roll(x, half, 1) * s2


EVEN_COLS = 2048


def _even_pre_kernel(x_ref, ada_ref, win_ref, qn_ref, wqn_ref, wqr_ref, wuk_ref, kvn_ref,
                     cm_ref, s1m_ref, s2m_ref, cr_ref, s1r_ref, s2r_ref,
                     q_ref, kc_ref, lat_ref, kr_ref, rq_ref, rk_ref, rv_ref, rg_ref):
    bB, bT, D = x_ref.shape
    R = bB * bT
    h = _modulate(x_ref, ada_ref, 1).reshape(R, D).astype(bf16)
    proj = _dot(h, win_ref[...])
    cqn = _rms(proj[:, 0:256], qn_ref[...]).astype(bf16)
    q_nope = _dot(cqn, wqn_ref[...]).astype(bf16)
    q_rope = _dot(cqn, wqr_ref[...])
    lat = _rms(proj[:, 256:384], kvn_ref[...])
    cm, s1m, s2m = cm_ref[...], s1m_ref[...], s2m_ref[...]
    half_m = MLA_ROPE // 2
    kr = _rope(proj[:, 384:512], cm, s1m, s2m, half_m)
    lat_ref[...] = lat.reshape(bB, bT, LANES)
    kr_ref[...] = kr[:, :MLA_ROPE].reshape(bB, bT, MLA_ROPE)
    kr_ones = jnp.where(lax.broadcasted_iota(jnp.int32, (1, LANES), 1) < MLA_ROPE, kr, 1.0)
    kc_ref[...] = jnp.concatenate([lat, kr_ones], axis=1).astype(bf16).reshape(bB, bT, 2 * LANES)
    for p in range(MLA_HEADS // 2):
        q_lat2 = _dot(q_nope[:, LANES * p:LANES * (p + 1)], wuk_ref[p])
        for hh in range(2):
            hd = 2 * p + hh
            qr = _rope(q_rope[:, LANES * hd:LANES * (hd + 1)], cm, s1m, s2m, half_m)
            qc = jnp.concatenate([q_lat2[:, LANES * hh:LANES * (hh + 1)], qr], axis=1)
            q_ref[:, hd] = (qc * MLA_SCALE).astype(bf16).reshape(bB, bT, 2 * LANES)
    cr, s1r, s2r = cr_ref[...], s1r_ref[...], s2r_ref[...]
    half_r = RET_DK // 2
    rq_ref[...] = _rope(proj[:, 512:768], cr, s1r, s2r, half_r).reshape(bB, bT, 256)
    rk_ref[...] = (_rope(proj[:, 768:1024], cr, s1r, s2r, half_r) * RET_DK ** -0.5).reshape(bB, bT, 256)
    rv_ref[...] = proj[:, 1024:1536].reshape(bB, bT, 512)
    rg_ref[...] = proj[:, 1536:2048].reshape(bB, bT, 512)


def _even_pre_call(x, ada, wts, tabs):
    B, T, D = x.shape
    bB, bT = _row_tiles(B, T, MIX_ROWS)
    R = bB * bT
    row = lambda b, t: (b, t, 0)
    full = lambda shape: pl.BlockSpec(shape, lambda b, t: (0,) * len(shape))
    tab_map = (lambda b, t: (t, 0)) if bB == 1 else (lambda b, t: (0, 0))
    tab = lambda w: pl.BlockSpec((R, w), tab_map)
    out3 = lambda w, dt: jax.ShapeDtypeStruct((B, T, w), dt)
    return pl.pallas_call(
        _even_pre_kernel,
        out_shape=(
            jax.ShapeDtypeStruct((B, MLA_HEADS, T, 256), bf16),
            out3(256, bf16), out3(MLA_KV_RANK, f32), out3(MLA_ROPE, f32),
            out3(256, f32), out3(256, f32), out3(512, f32), out3(512, f32),
        ),
        grid=(B // bB, T // bT),
        in_specs=[
            pl.BlockSpec((bB, bT, D), row),
            pl.BlockSpec((bB, 9, D), lambda b, t: (b, 0, 0)),
            full((D, EVEN_COLS)), full((1, 256)), full((256, 512)), full((256, 1024)),
            full((4, LANES, 256)), full((1, LANES)),
            tab(LANES), tab(LANES), tab(LANES), tab(256), tab(256), tab(256),
        ],
        out_specs=(
            pl.BlockSpec((bB, MLA_HEADS, bT, 256), lambda b, t: (b, 0, t, 0)),
            pl.BlockSpec((bB, bT, 256), row), pl.BlockSpec((bB, bT, MLA_KV_RANK), row),
            pl.BlockSpec((bB, bT, MLA_ROPE), row),
            pl.BlockSpec((bB, bT, 256), row), pl.BlockSpec((bB, bT, 256), row),
            pl.BlockSpec((bB, bT, 512), row), pl.BlockSpec((bB, bT, 512), row),
        ),
        compiler_params=_cparams("parallel", "parallel"),
        name="even_pre",
    )(x, ada, wts["w_in"], wts["q_norm"], wts["w_q_nope"], wts["w_q_rope"], wts["w_uk_bd"], wts["kv_norm"], *tabs)


MLA_SCALE = (MLA_NOPE + MLA_ROPE) ** -0.5


def _mla_prompt_kernel(q_ref, kc_ref, o_ref, m_sc, acc_sc, *, tq):
    qi = pl.program_id(1)
    m_sc[...] = jnp.full_like(m_sc, -jnp.inf)
    acc_sc[...] = jnp.zeros_like(acc_sc)

    def block(j, masked):
        kblk = kc_ref[0, pl.ds(pl.multiple_of(j * tq, tq), tq), :]
        for h in range(MLA_HEADS):
            rows = slice(h * tq, (h + 1) * tq)
            s = _dot_nt(q_ref[0, h], kblk)
            if masked:
                qpos = lax.broadcasted_iota(jnp.int32, (tq, tq), 0)
                kpos = lax.broadcasted_iota(jnp.int32, (tq, tq), 1)
                s = jnp.where(kpos <= qpos, s, -jnp.inf)
            m_prev = m_sc[rows, :]
            m_new = jnp.maximum(m_prev, jnp.broadcast_to(jnp.max(s, axis=-1, keepdims=True), (tq, LANES)))
            alpha = jnp.exp(m_prev - m_new)
            p = jnp.exp(s - jnp.concatenate([m_new] * (tq // LANES), axis=1))
            acc_sc[rows, :] = jnp.concatenate([alpha, alpha], axis=1) * acc_sc[rows, :] + _dot(p.astype(bf16), kblk)
            m_sc[rows, :] = m_new

    def body(j, carry):
        block(j, False)
        return carry

    lax.fori_loop(0, qi, body, 0)
    block(qi, True)
    for h in range(MLA_HEADS):
        acc = acc_sc[h * tq:(h + 1) * tq, :]
        o_ref[0, :, LANES * h:LANES * (h + 1)] = (acc[:, :LANES] / acc[:, 2 * LANES - 1:2 * LANES]).astype(bf16)


def _mla_prompt_call(q, kc):
    B, H, T, W = q.shape
    tq = min(ATT_TQ, T)
    R = H * tq
    return pl.pallas_call(
        functools.partial(_mla_prompt_kernel, tq=tq),
        out_shape=jax.ShapeDtypeStruct((B, T, H * LANES), bf16),
        grid=(B, T // tq),
        in_specs=[
            pl.BlockSpec((1, H, tq, W), lambda b, i: (b, 0, i, 0)),
            pl.BlockSpec((1, T, W), lambda b, i: (b, 0, 0)),
        ],
        out_specs=pl.BlockSpec((1, tq, H * LANES), lambda b, i: (b, i, 0)),
        scratch_shapes=[pltpu.VMEM((R, LANES), f32), pltpu.VMEM((R, W), f32)],
        compiler_params=_cparams("parallel", "parallel"),
        name="mla_prompt",
    )(q, kc)


def _mla_sample_kernel(pt_ref, q_ref, kc_ref, lat_hbm, krt_hbm, o_ref, lat_buf, krt_buf, sem, *, T, layer):
    b = pl.program_id(0)
    n_pages = pt_ref.shape[1]
    R = MLA_HEADS * T
    slot = b % 2

    def page_copies(seq, dst, pg):
        page = pt_ref[seq, pg]
        span = pl.ds(pl.multiple_of(pg * PAGE_SIZE, PAGE_SIZE), PAGE_SIZE)
        return (pltpu.make_async_copy(lat_hbm.at[layer, page], lat_buf.at[dst, span, :], sem.at[dst, 0]),
                pltpu.make_async_copy(krt_hbm.at[layer, page], krt_buf.at[dst, :, span], sem.at[dst, 1]))

    def for_pages(seq, dst, act):
        def body(pg, carry):
            for cp in page_copies(seq, dst, pg):
                act(cp)
            return carry
        lax.fori_loop(0, n_pages, body, 0)

    @pl.when(b == 0)
    def _():
        for_pages(0, 0, lambda cp: cp.start())

    @pl.when(b + 1 < pl.num_programs(0))
    def _():
        for_pages(b + 1, 1 - slot, lambda cp: cp.start())

    for_pages(b, slot, lambda cp: cp.wait())

    q = q_ref[0].reshape(R, 2 * LANES)
    lat = lat_buf[slot].astype(bf16)
    krt = krt_buf[slot].astype(bf16)
    s = _dot_nt(q[:, :LANES], lat) + _dot(q[:, LANES:LANES + MLA_ROPE], krt)
    kc = kc_ref[0]
    qpos = lax.broadcasted_iota(jnp.int32, (R, T), 0) % T
    kpos = lax.broadcasted_iota(jnp.int32, (R, T), 1)
    s_new = jnp.where(kpos <= qpos, _dot_nt(q, kc), -jnp.inf)
    m = jnp.maximum(jnp.max(s, axis=-1, keepdims=True), jnp.max(s_new, axis=-1, keepdims=True))
    p = jnp.exp(s - m)
    p_new = jnp.exp(s_new - m)
    l = jnp.sum(p, axis=-1, keepdims=True) + jnp.sum(p_new, axis=-1, keepdims=True)
    o = ((_dot(p.astype(bf16), lat) + _dot(p_new.astype(bf16), kc[:, :LANES])) / l).astype(bf16)
    for h in range(MLA_HEADS):
        o_ref[0, :, LANES * h:LANES * (h + 1)] = o[h * T:(h + 1) * T]


def _mla_sample_call(q, kc, cache_lat, cache_krt, page_table, layer):
    B, H, T, W = q.shape
    n_keys = page_table.shape[1] * PAGE_SIZE
    grid_spec = pltpu.PrefetchScalarGridSpec(
        num_scalar_prefetch=1,
        grid=(B,),
        in_specs=[
            pl.BlockSpec((1, H, T, W), lambda b, pt: (b, 0, 0, 0)),
            pl.BlockSpec((1, T, W), lambda b, pt: (b, 0, 0)),
            pl.BlockSpec(memory_space=pl.ANY),
            pl.BlockSpec(memory_space=pl.ANY),
        ],
        out_specs=pl.BlockSpec((1, T, H * LANES), lambda b, pt: (b, 0, 0)),
        scratch_shapes=[
            pltpu.VMEM((2, n_keys, MLA_KV_RANK), f32),
            pltpu.VMEM((2, MLA_ROPE, n_keys), f32),
            pltpu.SemaphoreType.DMA((2, 2)),
        ],
    )
    return pl.pallas_call(
        functools.partial(_mla_sample_kernel, T=T, layer=layer),
        out_shape=jax.ShapeDtypeStruct((B, T, H * LANES), bf16),
        grid_spec=grid_spec,
        compiler_params=_cparams("arbitrary"),
        name="mla_sample",
    )(page_table, q, kc, cache_lat, cache_krt)


RET_LOG_GAMMA = tuple(float(np.log1p(-np.exp2(-5.0 - h))) for h in range(RET_HEADS))


def _ret_kernel(q_ref, k_ref, v_ref, s0_ref, o_ref, sf_ref, s_sc, *, L):
    c = pl.program_id(1)
    HK = RET_HEADS * RET_DK

    @pl.when(c == 0)
    def _():
        s_sc[...] = s0_ref[0].reshape(HK, RET_DV)

    q, k, v = q_ref[0], k_ref[0], v_ref[0]
    ti = lax.broadcasted_iota(jnp.int32, (L, L), 0)
    ji = lax.broadcasted_iota(jnp.int32, (L, L), 1)
    causal = ti >= ji
    dist = (ti - ji).astype(f32)
    tcol = lax.broadcasted_iota(jnp.int32, (L, 1), 0).astype(f32)
    lane_head = lax.broadcasted_iota(jnp.int32, (1, HK), 1) // RET_DK
    row_head = lax.broadcasted_iota(jnp.int32, (HK, 1), 0) // RET_DK
    kb = k.astype(bf16)
    s = s_sc[...]
    sb = s.astype(bf16)
    upd = jnp.zeros((HK, RET_DV), f32)
    row_decay = jnp.zeros((HK, 1), f32)
    outs = []
    for h in range(RET_HEADS):
        lg = RET_LOG_GAMMA[h]
        mh = lane_head == h
        qh = jnp.where(mh, q, 0.0)
        decay = jnp.exp(jnp.where(causal, dist * lg, -jnp.inf))
        p = (_dot_nt(qh.astype(bf16), kb) * decay).astype(bf16)
        vh = v[:, RET_DV * h:RET_DV * (h + 1)].astype(bf16)
        o = _dot(p, vh) + _dot((qh * jnp.exp((tcol + 1.0) * lg)).astype(bf16), sb)
        outs.append(o)
        kt = jnp.where(mh, k, 0.0) * jnp.exp((L - 1.0 - tcol) * lg)
        upd = upd + _dot_tn(kt.astype(bf16), vh)
        row_decay = jnp.where(row_head == h, math.exp(L * lg), row_decay)
    o_ref[0] = jnp.concatenate(outs, axis=1)
    s_new = row_decay * s + upd
    s_sc[...] = s_new

    @pl.when(c == pl.num_programs(1) - 1)
    def _():
        sf_ref[0] = s_new.reshape(RET_HEADS, RET_DK, RET_DV)


def _ret_call(rq, rk, rv, s0_all, layer):
    B, T, _ = rq.shape
    L = min(SCAN_CHUNK, T)
    HK, HV = RET_HEADS * RET_DK, RET_HEADS * RET_DV
    row = lambda b, c: (b, c, 0)
    st_shape = (RET_HEADS, RET_DK, RET_DV)
    st_in = pl.BlockSpec((None, 1) + st_shape, lambda b, c: (layer, b, 0, 0, 0))
    st = pl.BlockSpec((1,) + st_shape, lambda b, c: (b, 0, 0, 0))
    return pl.pallas_call(
        functools.partial(_ret_kernel, L=L),
        out_shape=(jax.ShapeDtypeStruct((B, T, HV), f32), jax.ShapeDtypeStruct((B,) + st_shape, f32)),
        grid=(B, T // L),
        in_specs=[pl.BlockSpec((1, L, HK), row), pl.BlockSpec((1, L, HK), row), pl.BlockSpec((1, L, HV), row), st_in],
        out_specs=(pl.BlockSpec((1, L, HV), row), st),
        scratch_shapes=[pltpu.VMEM((HK, RET_DV), f32)],
        compiler_params=_cparams("parallel", "arbitrary"),
        name="retention",
    )(rq, rk, rv, s0_all)


def _head_rms(x, g, width):
    return jnp.concatenate([_rms(x[:, i:i + width], g) for i in range(0, x.shape[1], width)], axis=1)


def _even_post_kernel(x_ref, ada_ref, ol_ref, ro_ref, rg_ref, wuv_ref, rn_ref, wo_ref, g_ref, b_ref, o_ref):
    bB, bT, D = x_ref.shape
    R = bB * bT
    mla_out = _dot(ol_ref[...].reshape(R, MLA_HEADS * LANES), wuv_ref[...])
    ret = _silu(rg_ref[...].reshape(R, 512)) * _head_rms(ro_ref[...].reshape(R, 512), rn_ref[...], RET_DV)
    cat = jnp.concatenate([mla_out, ret], axis=1).astype(bf16)
    mix = _dot(cat, wo_ref[...])
    o_ref[...] = _residual_ln(x_ref, ada_ref, 1, 1.0, mix, g_ref, b_ref)


def _even_post_call(x, ada, o_lat, ro, rg, wts, ln_g, ln_b, layer):
    B, T, D = x.shape
    bB, bT = _row_tiles(B, T, MIX_ROWS)
    row = lambda b, t: (b, t, 0)
    full = lambda shape: pl.BlockSpec(shape, lambda b, t: (0,) * len(shape))
    ln = pl.BlockSpec((None, None, 1, D), lambda b, t: (layer, 1, 0, 0))
    return pl.pallas_call(
        _even_post_kernel,
        out_shape=jax.ShapeDtypeStruct((B, T, D), f32),
        grid=(B // bB, T // bT),
        in_specs=[
            pl.BlockSpec((bB, bT, D), row), pl.BlockSpec((bB, 9, D), lambda b, t: (b, 0, 0)),
            pl.BlockSpec((bB, bT, MLA_HEADS * LANES), row), pl.BlockSpec((bB, bT, 512), row),
            pl.BlockSpec((bB, bT, 512), row),
            full((MLA_HEADS * LANES, MLA_HEADS * MLA_V)), full((1, RET_DV)), full((1024, D)), ln, ln,
        ],
        out_specs=pl.BlockSpec((bB, bT, D), row),
        compiler_params=_cparams("parallel", "parallel"),
        name="even_post",
    )(x, ada, o_lat, ro, rg, wts["w_uv_bd"], wts["ret_norm"], wts["w_out"], ln_g, ln_b)


ODD_COLS = 4352
ODD_OFF = dict(gq=0, gk=256, gv=512, glr=1024, gr=1152, z=1664, xbc=2688, dt=4224)


def _odd_pre_kernel(x_ref, ada_ref, win_ref, wg_ref, bg_ref, cw_ref, cb_ref, dtb_ref, c0_ref,
                    gq_ref, gk_ref, gv_ref, la_ref, gr_ref, z_ref, xs_ref, bm_ref, cm_ref, dt_ref, cs_ref, buf):
    bB, bT, D = x_ref.shape
    R = bB * bT
    t = pl.program_id(1)
    O = ODD_OFF
    h = _modulate(x_ref, ada_ref, 1).reshape(R, D).astype(bf16)
    proj = _dot(h, win_ref[...])
    sl = lambda name, w: proj[:, O[name]:O[name] + w]
    gq_ref[...] = (sl("gq", 256) * GLA_DK ** -0.5).reshape(bB, bT, 256)
    gk_ref[...] = sl("gk", 256).reshape(bB, bT, 256)
    gv_ref[...] = sl("gv", 512).reshape(bB, bT, 512)
    gr_ref[...] = sl("gr", 512).reshape(bB, bT, 512)
    z_ref[...] = sl("z", 1024).reshape(bB, bT, 1024)
    gate_pre = _dot(sl("glr", LANES).astype(bf16), wg_ref[...]) + bg_ref[...]
    la_ref[...] = (_log_sigmoid(gate_pre) / GLA_TAU).reshape(bB, bT, 256)
    dt_ref[...] = _softplus(sl("dt", SSM_HEADS) + dtb_ref[...]).reshape(bB, bT, SSM_HEADS)

    W = SSM_CONV - 1
    lo = SUBLANES - W

    @pl.when(t == 0)
    def _():
        buf[:, lo:SUBLANES, :] = c0_ref[...]

    xbc = sl("xbc", SSM_CONV_DIM).reshape(bB, bT, SSM_CONV_DIM)
    buf[:, SUBLANES:SUBLANES + bT, :] = xbc
    cw = cw_ref[...]
    conv = cb_ref[...] + sum(buf[:, lo + i:lo + i + bT, :] * cw[i:i + 1, :] for i in range(SSM_CONV))
    tail = buf[:, bT + lo:bT + SUBLANES, :]
    buf[:, lo:SUBLANES, :] = tail

    @pl.when(t == pl.num_programs(1) - 1)
    def _():
        cs_ref[...] = tail

    act = _silu(conv)
    xs_ref[...] = act[:, :, :SSM_INNER]
    bm_ref[...] = act[:, :, SSM_INNER:SSM_INNER + 256]
    cm_ref[...] = act[:, :, SSM_INNER + 256:]


def _odd_pre_call(x, ada, wts, conv0_all, layer):
    B, T, D = x.shape
    bB, bT = _row_tiles(B, T, MIX_ROWS)
    row = lambda b, t: (b, t, 0)
    full = lambda shape: pl.BlockSpec(shape, lambda b, t: (0,) * len(shape))
    out3 = lambda w: jax.ShapeDtypeStruct((B, T, w), f32)
    blk = lambda w: pl.BlockSpec((bB, bT, w), row)
    widths = (256, 256, 512, 256, 512, 1024, SSM_INNER, 256, 256, SSM_HEADS)
    cs = pl.BlockSpec((bB, SSM_CONV - 1, SSM_CONV_DIM), lambda b, t: (b, 0, 0))
    cs_in = pl.BlockSpec((None, bB, SSM_CONV - 1, SSM_CONV_DIM), lambda b, t: (layer, b, 0, 0))
    return pl.pallas_call(
        _odd_pre_kernel,
        out_shape=tuple(out3(w) for w in widths) + (jax.ShapeDtypeStruct(conv0_all.shape[1:], f32),),
        grid=(B // bB, T // bT),
        in_specs=[
            blk(D), pl.BlockSpec((bB, 9, D), lambda b, t: (b, 0, 0)),
            full((D, ODD_COLS)), full((LANES, 256)), full((1, 256)),
            full((SSM_CONV, SSM_CONV_DIM)), full((1, SSM_CONV_DIM)), full((1, SSM_HEADS)), cs_in,
        ],
        out_specs=tuple(blk(w) for w in widths) + (cs,),
        scratch_shapes=[pltpu.VMEM((bB, SUBLANES + bT, SSM_CONV_DIM), f32)],
        compiler_params=_cparams("parallel", "arbitrary"),
        name="odd_pre",
    )(x, ada, wts["w_in"], wts["w_gate"], wts["b_gate"], wts["conv_w"], wts["conv_b"], wts["dt_bias"], conv0_all)


def _gla_kernel(q_ref, k_ref, la_ref, v_ref, s0_ref, sel_ref, o_ref, sf_ref, s_sc, *, L, sb):
    c = pl.program_id(1)
    HK, HV = GLA_HEADS * GLA_DK, GLA_HEADS * GLA_DV
    row_head = lax.broadcasted_iota(jnp.int32, (HK, HV), 0) // GLA_DK
    col_head = lax.broadcasted_iota(jnp.int32, (HK, HV), 1) // GLA_DV
    lane_hk = lax.broadcasted_iota(jnp.int32, (1, HK), 1) // GLA_DK
    lane_hv = lax.broadcasted_iota(jnp.int32, (1, HV), 1) // GLA_DV

    @pl.when(c == 0)
    def _():
        s_sc[...] = jnp.zeros_like(s_sc)
        for h in range(GLA_HEADS):
            s_sc[GLA_DK * h:GLA_DK * (h + 1), GLA_DV * h:GLA_DV * (h + 1)] = s0_ref[0, h]

    q, k, la, v = q_ref[0], k_ref[0], la_ref[0], v_ref[0]
    vb = v.astype(bf16)
    ti = lax.broadcasted_iota(jnp.int32, (L, L), 0)
    ji = lax.broadcasted_iota(jnp.int32, (L, L), 1)
    seg = _dot((ti >= ji).astype(f32), la, precision=HIGHEST)
    s = s_sc[...]
    o_inter = _dot((q * jnp.exp(seg)).astype(bf16), s.astype(bf16))
    t3 = lax.broadcasted_iota(jnp.int32, (sb, sb, 1), 0)
    j3 = lax.broadcasted_iota(jnp.int32, (sb, sb, 1), 1)
    for i in range(L // sb):
        r0 = i * sb
        qi, ki, si, vi = q[r0:r0 + sb], k[r0:r0 + sb], seg[r0:r0 + sb], v[r0:r0 + sb]
        diff = si[:, None, :] - si[None, :, :]
        e = qi[:, None, :] * ki[None, :, :] * jnp.exp(jnp.where(t3 >= j3, diff, -jnp.inf))
        sc = _dot(e.reshape(sb * sb, HK).astype(bf16), sel_ref[...])
        o = o_inter[r0:r0 + sb] + jnp.sum(sc.reshape(sb, sb, HV) * vi[None, :, :], axis=1)
        if i > 0:
            segp = seg[r0 - 1:r0]
            qt = qi * jnp.exp(si - segp)
            kt = k[:r0] * jnp.exp(segp - seg[:r0])
            qs = jnp.concatenate([jnp.where(lane_hk == h, qt, 0.0) for h in range(GLA_HEADS)], axis=0)
            pr = _dot_nt(qs.astype(bf16), kt.astype(bf16)).astype(bf16)
            full = _dot(pr, vb[:r0])
            for h in range(GLA_HEADS):
                o = o + jnp.where(lane_hv == h, full[h * sb:(h + 1) * sb], 0.0)
        o_ref[0, r0:r0 + sb, :] = o
    seg_last = seg[L - 1:L]
    kt = (k * jnp.exp(seg_last - seg)).astype(bf16)
    upd = _dot_tn(kt, vb)
    tot = _dot_tn(la, jnp.ones((L, GLA_DV), f32), precision=HIGHEST)
    row_decay = jnp.concatenate([jnp.exp(tot)] * GLA_HEADS, axis=1)
    s_new = jnp.where(row_head == col_head, row_decay * s + upd, 0.0)
    s_sc[...] = s_new

    @pl.when(c == pl.num_programs(1) - 1)
    def _():
        for h in range(GLA_HEADS):
            sf_ref[0, h] = s_new[GLA_DK * h:GLA_DK * (h + 1), GLA_DV * h:GLA_DV * (h + 1)]


def _gla_call(gq, gk, la, gv, s0_all, sel, layer):
    B, T, _ = gq.shape
    L = min(GLA_CHUNK, T)
    sb = min(GLA_SUB, L)
    HK, HV = GLA_HEADS * GLA_DK, GLA_HEADS * GLA_DV
    row = lambda b, c: (b, c, 0)
    st_shape = (GLA_HEADS, GLA_DK, GLA_DV)
    st_in = pl.BlockSpec((None, 1) + st_shape, lambda b, c: (layer, b, 0, 0, 0))
    st = pl.BlockSpec((1,) + st_shape, lambda b, c: (b, 0, 0, 0))
    return pl.pallas_call(
        functools.partial(_gla_kernel, L=L, sb=sb),
        out_shape=(jax.ShapeDtypeStruct((B, T, HV), f32), jax.ShapeDtypeStruct((B,) + st_shape, f32)),
        grid=(B, T // L),
        in_specs=[pl.BlockSpec((1, L, HK), row)] * 3 + [pl.BlockSpec((1, L, HV), row), st_in,
                  pl.BlockSpec((HK, HV), lambda b, c: (0, 0))],
        out_specs=(pl.BlockSpec((1, L, HV), row), st),
        scratch_shapes=[pltpu.VMEM((HK, HV), f32)],
        compiler_params=_cparams("parallel", "arbitrary"),
        name="gla",
    )(gq, gk, la, gv, s0_all, sel)


def _ssd_kernel(xs_ref, bm_ref, cm_ref, dt_ref, a_ref, d_ref, s0_ref, y_ref, sf_ref, s_sc, *, L):
    c = pl.program_id(1)
    P, N = SSM_HEADDIM, SSM_DSTATE
    pairs = SSM_HEADS // 2

    @pl.when(c == 0)
    def _():
        for i in range(pairs):
            s_sc[i] = s0_ref[0, 2 * i:2 * i + 2].reshape(2 * P, N)

    dt = dt_ref[0]
    la = dt * a_ref[...]
    ti = lax.broadcasted_iota(jnp.int32, (L, L), 0)
    ji = lax.broadcasted_iota(jnp.int32, (L, L), 1)
    causal = ti >= ji
    seg = _dot(causal.astype(f32), la, precision=HIGHEST)
    seg_t = _dot_tn(la, (ti <= ji).astype(f32), precision=HIGHEST)
    seg_last = seg[L - 1:L]
    lo = lax.broadcasted_iota(jnp.int32, (1, 2 * P), 1) < P
    row_lo = lax.broadcasted_iota(jnp.int32, (2 * P, 1), 0) < P
    for g in range(SSM_GROUPS):
        cg = cm_ref[0, :, N * g:N * (g + 1)].astype(bf16)
        bg = bm_ref[0, :, N * g:N * (g + 1)].astype(bf16)
        gmat = _dot_nt(cg, bg)
        for i in range(g * pairs // SSM_GROUPS, (g + 1) * pairs // SSM_GROUPS):
            ha, hb = 2 * i, 2 * i + 1
            xp = xs_ref[0, :, 2 * P * i:2 * P * (i + 1)]
            xdt = xp * jnp.where(lo, dt[:, ha:ha + 1], dt[:, hb:hb + 1])
            xdt_b = xdt.astype(bf16)
            pa = (gmat * jnp.exp(jnp.where(causal, seg[:, ha:ha + 1] - seg_t[ha:ha + 1, :], -jnp.inf))).astype(bf16)
            pb = (gmat * jnp.exp(jnp.where(causal, seg[:, hb:hb + 1] - seg_t[hb:hb + 1, :], -jnp.inf))).astype(bf16)
            y = _dot(pa, jnp.where(lo, xdt_b, jnp.zeros_like(xdt_b))) + _dot(pb, jnp.where(lo, jnp.zeros_like(xdt_b), xdt_b))
            sp = s_sc[i]
            y = y + _dot_nt(cg, sp.astype(bf16)) * jnp.where(lo, jnp.exp(seg[:, ha:ha + 1]), jnp.exp(seg[:, hb:hb + 1]))
            y_ref[0, :, 2 * P * i:2 * P * (i + 1)] = y + d_ref[:, 2 * P * i:2 * P * (i + 1)] * xp
            tail = jnp.where(lo, jnp.exp(seg_last[:, ha:ha + 1] - seg[:, ha:ha + 1]),
                             jnp.exp(seg_last[:, hb:hb + 1] - seg[:, hb:hb + 1]))
            chunk_decay = jnp.where(row_lo, jnp.exp(seg_last[:, ha:ha + 1]), jnp.exp(seg_last[:, hb:hb + 1]))
            sp_new = chunk_decay * sp + _dot_tn((xdt * tail).astype(bf16), bg)
            s_sc[i] = sp_new

            @pl.when(c == pl.num_programs(1) - 1)
            def _():
                sf_ref[0, ha:hb + 1] = sp_new.reshape(2, P, N)


def _ssd_call(xs, bm, cm, dt, a_row, d_row, s0_t, layer):
    B, T, _ = xs.shape
    L = min(SCAN_CHUNK, T)
    row = lambda b, c: (b, c, 0)
    st_shape = (SSM_HEADS, SSM_HEADDIM, SSM_DSTATE)
    return pl.pallas_call(
        functools.partial(_ssd_kernel, L=L),
        out_shape=(jax.ShapeDtypeStruct((B, T, SSM_INNER), f32), jax.ShapeDtypeStruct((B,) + st_shape, f32)),
        grid=(B, T // L),
        in_specs=[
            pl.BlockSpec((1, L, SSM_INNER), row), pl.BlockSpec((1, L, 256), row), pl.BlockSpec((1, L, 256), row),
            pl.BlockSpec((1, L, SSM_HEADS), row),
            pl.BlockSpec((1, SSM_HEADS), lambda b, c: (0, 0)), pl.BlockSpec((1, SSM_INNER), lambda b, c: (0, 0)),
            pl.BlockSpec((None, 1) + st_shape, lambda b, c: (layer, b, 0, 0, 0)),
        ],
        out_specs=(pl.BlockSpec((1, L, SSM_INNER), row), pl.BlockSpec((1,) + st_shape, lambda b, c: (b, 0, 0, 0))),
        scratch_shapes=[pltpu.VMEM((SSM_HEADS // 2, 2 * SSM_HEADDIM, SSM_DSTATE), f32)],
        compiler_params=_cparams("parallel", "arbitrary"),
        name="ssd",
    )(xs, bm, cm, dt, a_row, d_row, s0_t)


def _odd_post_kernel(x_ref, ada_ref, go_ref, gr_ref, y_ref, z_ref, gn_ref, sn_ref, wo_ref, g_ref, b_ref, o_ref):
    bB, bT, D = x_ref.shape
    R = bB * bT
    gla = _silu(gr_ref[...].reshape(R, 512)) * _head_rms(go_ref[...].reshape(R, 512), gn_ref[...], GLA_DV)
    yz = y_ref[...].reshape(R, SSM_INNER) * _silu(z_ref[...].reshape(R, SSM_INNER))
    gw = SSM_INNER // SSM_GROUPS
    sn = sn_ref[...]
    yn = jnp.concatenate([_rms(yz[:, gw * g:gw * (g + 1)], sn[:, gw * g:gw * (g + 1)]) for g in range(SSM_GROUPS)], axis=1)
    cat = jnp.concatenate([gla, yn], axis=1).astype(bf16)
    mix = _dot(cat, wo_ref[...])
    o_ref[...] = _residual_ln(x_ref, ada_ref, 1, 1.0, mix, g_ref, b_ref)


def _odd_post_call(x, ada, go, gr, y, z, wts, ln_g, ln_b, layer):
    B, T, D = x.shape
    bB, bT = _row_tiles(B, T, MIX_ROWS)
    row = lambda b, t: (b, t, 0)
    full = lambda shape: pl.BlockSpec(shape, lambda b, t: (0,) * len(shape))
    blk = lambda w: pl.BlockSpec((bB, bT, w), row)
    ln = pl.BlockSpec((None, None, 1, D), lambda b, t: (layer, 1, 0, 0))
    return pl.pallas_call(
        _odd_post_kernel,
        out_shape=jax.ShapeDtypeStruct((B, T, D), f32),
        grid=(B // bB, T // bT),
        in_specs=[
            blk(D), pl.BlockSpec((bB, 9, D), lambda b, t: (b, 0, 0)),
            blk(512), blk(512), blk(SSM_INNER), blk(SSM_INNER),
            full((1, GLA_DV)), full((1, SSM_INNER)), full((512 + SSM_INNER, D)), ln, ln,
        ],
        out_specs=blk(D),
        compiler_params=_cparams("parallel", "parallel"),
        name="odd_post",
    )(x, ada, go, gr, y, z, wts["gla_norm"], wts["ssm_norm"], wts["w_out"], ln_g, ln_b)


def _pad_cols(w, width):
    return jnp.pad(w, ((0, 0), (0, width - w.shape[1])))


def _prep_even(j, w_in_even, mla_q_norm, w_uq, mla_kv_norm, w_uk, w_uv, ret_norm, w_out_even):
    w = w_in_even[j]
    k0 = MLA_Q_RANK + MLA_KV_RANK
    w_in = jnp.concatenate([w[:, :k0], _pad_cols(w[:, k0:k0 + MLA_ROPE], LANES), w[:, k0 + MLA_ROPE:]], axis=1)
    uq = w_uq[j].reshape(MLA_Q_RANK, MLA_HEADS, MLA_NOPE + MLA_ROPE)
    w_q_nope = uq[:, :, :MLA_NOPE].reshape(MLA_Q_RANK, MLA_HEADS * MLA_NOPE)
    w_q_rope = jnp.pad(uq[:, :, MLA_NOPE:], ((0, 0), (0, 0), (0, LANES - MLA_ROPE))).reshape(MLA_Q_RANK, MLA_HEADS * LANES)
    eye2 = jnp.eye(2, dtype=f32)
    uk = jnp.transpose(w_uk[j], (1, 2, 0)).reshape(MLA_HEADS // 2, 2, MLA_NOPE, MLA_KV_RANK)
    w_uk_bd = (uk[:, :, :, None, :] * eye2[None, :, None, :, None]).reshape(MLA_HEADS // 2, 2 * MLA_NOPE, 2 * MLA_KV_RANK)
    eye8 = jnp.eye(MLA_HEADS, dtype=f32)
    uv = jnp.transpose(w_uv[j], (1, 0, 2))
    w_uv_bd = (uv[:, :, None, :] * eye8[:, None, :, None]).reshape(MLA_HEADS * MLA_KV_RANK, MLA_HEADS * MLA_V)
    return dict(
        w_in=w_in.astype(bf16), q_norm=mla_q_norm[j][None, :], w_q_nope=w_q_nope.astype(bf16),
        w_q_rope=w_q_rope.astype(bf16), w_uk_bd=w_uk_bd.astype(bf16), kv_norm=mla_kv_norm[j][None, :],
        w_uv_bd=w_uv_bd.astype(bf16), ret_norm=ret_norm[j][None, :], w_out=w_out_even[j].astype(bf16),
    )


def _prep_odd(j, w_in_odd, gla_w_gate, gla_b_gate, gla_norm, ssm_conv_w, ssm_conv_b, ssm_dt_bias, ssm_a_log, ssm_d,
              ssm_norm, w_out_odd):
    w = w_in_odd[j]
    c_glr = 2 * GLA_HEADS * GLA_DK + GLA_HEADS * GLA_DV
    c_dt = w.shape[1] - SSM_HEADS
    w_in = jnp.concatenate([
        w[:, :c_glr], _pad_cols(w[:, c_glr:c_glr + GLA_GATE_RANK], LANES), w[:, c_glr + GLA_GATE_RANK:c_dt],
        _pad_cols(w[:, c_dt:], LANES)], axis=1)
    w_gate = jnp.pad(gla_w_gate[j], ((0, LANES - GLA_GATE_RANK), (0, 0)))
    return dict(
        w_in=w_in.astype(bf16), w_gate=w_gate.astype(bf16), b_gate=gla_b_gate[j][None, :],
        conv_w=ssm_conv_w[j], conv_b=ssm_conv_b[j][None, :], dt_bias=ssm_dt_bias[j][None, :],
        a_row=-jnp.exp(ssm_a_log[j].astype(f32))[None, :], d_row=jnp.repeat(ssm_d[j], SSM_HEADDIM)[None, :],
        gla_norm=gla_norm[j][None, :], ssm_norm=ssm_norm[j][None, :], w_out=w_out_odd[j].astype(bf16),
    )


def _gla_select():
    r = np.arange(GLA_HEADS * GLA_DK)[:, None] // GLA_DK
    c = np.arange(GLA_HEADS * GLA_DV)[None, :] // GLA_DV
    return jnp.asarray(r == c, dtype=bf16)


def _trunk(x, ada_all, pos, p, ret_s0, gla_s0, ssm_s0_t, conv0, cache_lat, cache_krt, page_table):
    B, T, D = x.shape
    bB, _ = _row_tiles(B, T, MIX_ROWS)
    tabs_m = _rope_tables(pos, MLA_ROPE // 2, LANES, MLA_ROPE)
    tabs_r = _rope_tables(pos, RET_DK // 2, RET_HEADS * RET_DK, RET_HEADS * RET_DK)
    if bB > 1:
        tabs_m = tuple(jnp.tile(a, (bB, 1)) for a in tabs_m)
        tabs_r = tuple(jnp.tile(a, (bB, 1)) for a in tabs_r)
    tabs = tabs_m + tabs_r
    sel = _gla_select()
    lat_l, kr_l, ret_l, gla_l, ssm_l, conv_l = [], [], [], [], [], []
    for l in range(DEPTH):
        ada = ada_all[l]
        j = l // 2
        x = _ffn_call(x, ada, p["ffn_w1"], p["ffn_w3"], p["ffn_w2"], p["ln_g"], p["ln_b"], l, 0)
        if l % 2 == 0:
            wts = p["even"][j]
            q, kc, lat, kr, rq, rk, rv, rg = _even_pre_call(x, ada, wts, tabs)
            if page_table is None:
                o_lat = _mla_prompt_call(q, kc)
            else:
                o_lat = _mla_sample_call(q, kc, cache_lat, cache_krt, page_table, j)
            ro, rs = _ret_call(rq, rk, rv, ret_s0, j)
            x = _even_post_call(x, ada, o_lat, ro, rg, wts, p["ln_g"], p["ln_b"], l)
            lat_l.append(lat)
            kr_l.append(kr)
            ret_l.append(rs)
        else:
            wts = p["odd"][j]
            gq, gk, gv, la, gr, z, xs, bm, cm, dt, cs = _odd_pre_call(x, ada, wts, conv0, j)
            go, gs = _gla_call(gq, gk, la, gv, gla_s0, sel, j)
            y, ss = _ssd_call(xs, bm, cm, dt, wts["a_row"], wts["d_row"], ssm_s0_t, j)
            x = _odd_post_call(x, ada, go, gr, y, z, wts, p["ln_g"], p["ln_b"], l)
            gla_l.append(gs)
            ssm_l.append(ss)
            conv_l.append(cs)
        x = _ffn_call(x, ada, p["ffn_w1"], p["ffn_w3"], p["ffn_w2"], p["ln_g"], p["ln_b"], l, 1)
    ssm_s = jnp.swapaxes(jnp.stack(ssm_l), -1, -2)
    return x, jnp.stack(lat_l), jnp.stack(kr_l), jnp.stack(ret_l), jnp.stack(gla_l), ssm_s, jnp.stack(conv_l)


def kernel(x_prompt, x_sample, cache_mla_latent, cache_mla_krope, page_table, state_retention, state_gla, state_ssm, state_conv, c_prompt, c_sample, w_ada, b_ada, ln_g, ln_b, ffn_w1, ffn_w3, ffn_w2, w_in_even, mla_q_norm, w_uq, mla_kv_norm, w_uk, w_uv, ret_norm, w_out_even, w_in_odd, gla_w_gate, gla_b_gate, gla_norm, ssm_conv_w, ssm_conv_b, ssm_dt_bias, ssm_a_log, ssm_d, ssm_norm, w_out_odd):
    n_even, n_odd = (DEPTH + 1) // 2, DEPTH // 2
    bp, tp, D = x_prompt.shape
    bs, ts, _ = x_sample.shape
    p = dict(
        ffn_w1=ffn_w1.astype(bf16), ffn_w3=ffn_w3.astype(bf16), ffn_w2=ffn_w2.astype(bf16),
        ln_g=ln_g.reshape(DEPTH, 3, 1, D), ln_b=ln_b.reshape(DEPTH, 3, 1, D),
        even=[_prep_even(j, w_in_even, mla_q_norm, w_uq, mla_kv_norm, w_uk, w_uv, ret_norm, w_out_even)
              for j in range(n_even)],
        odd=[_prep_odd(j, w_in_odd, gla_w_gate, gla_b_gate, gla_norm, ssm_conv_w, ssm_conv_b, ssm_dt_bias,
                       ssm_a_log, ssm_d, ssm_norm, w_out_odd) for j in range(n_odd)],
    )
    ada_all = _ada_call(jnp.concatenate([c_prompt, c_sample], axis=0), w_ada, b_ada)
    ada_p = ada_all[:, :bp].reshape(DEPTH, bp, 9, D)
    ada_s = ada_all[:, bp:].reshape(DEPTH, bs, 9, D)
    past_len = page_table.shape[1] * PAGE_SIZE
    pos_p = jnp.arange(tp, dtype=jnp.int32)
    pos_s = past_len + jnp.arange(ts, dtype=jnp.int32)
    ret0 = jnp.zeros((n_even, bp) + state_retention.shape[2:], f32)
    gla0 = jnp.zeros((n_odd, bp) + state_gla.shape[2:], f32)
    ssm0_t = jnp.zeros((n_odd, bp, SSM_HEADS, SSM_HEADDIM, SSM_DSTATE), f32)
    conv0 = jnp.zeros((n_odd, bp) + state_conv.shape[2:], f32)
    out_p = _trunk(x_prompt, ada_p, pos_p, p, ret0, gla0, ssm0_t, conv0, None, None, None)
    out_s = _trunk(x_sample, ada_s, pos_s, p, state_retention, state_gla, jnp.swapaxes(state_ssm, -1, -2), state_conv,
                   cache_mla_latent, jnp.swapaxes(cache_mla_krope, -1, -2), page_table)
    return tuple(a for pair in zip(out_p, out_s) for a in pair)
```

```python
import functools
import math

import jax
import jax.numpy as jnp
import numpy as np
from jax import lax
from jax.experimental import pallas as pl
from jax.experimental.pallas import tpu as pltpu

f32 = jnp.float32
bf16 = jnp.bfloat16
HIGHEST = lax.Precision.HIGHEST

DEPTH = 4
ALPHA = (2.0 * DEPTH) ** 0.25
ROPE_BASE = 10000.0
PAGE_SIZE = 128
MLA_HEADS, MLA_Q_RANK, MLA_KV_RANK, MLA_NOPE, MLA_ROPE, MLA_V = 8, 256, 128, 64, 32, 64
RET_HEADS, RET_DK, RET_DV = 4, 64, 128
GLA_HEADS, GLA_DK, GLA_DV, GLA_GATE_RANK, GLA_TAU = 4, 64, 128, 16, 16.0
SSM_HEADS, SSM_HEADDIM, SSM_GROUPS, SSM_DSTATE, SSM_CONV = 16, 64, 2, 128, 4
SSM_INNER = SSM_HEADS * SSM_HEADDIM
SSM_CONV_DIM = SSM_INNER + 2 * SSM_GROUPS * SSM_DSTATE
LANES = 128
SUBLANES = 8

FFN_ROWS = 512
MIX_ROWS = 512
ATT_TQ = 512
SCAN_CHUNK = 256
SCAN_ROWS = 256
SCAN_SEQS = 8
SSD_SEQS = 4
GLA_CHUNK = 64
GLA_SUB = 16
VMEM_LIMIT = 56 * 1024 * 1024


def _cparams(*sem):
    return pltpu.CompilerParams(dimension_semantics=sem, vmem_limit_bytes=VMEM_LIMIT)


def _row_tiles(B, T, rows):
    if T >= rows:
        return 1, rows
    return min(B, rows // T), T


def _dot(a, b, precision=None):
    return jnp.dot(a, b, preferred_element_type=f32, precision=precision)


def _dot_nt(a, b):
    return lax.dot_general(a, b, (((1,), (1,)), ((), ())), preferred_element_type=f32)


def _dot_tn(a, b, precision=None):
    return lax.dot_general(a, b, (((0,), (0,)), ((), ())), preferred_element_type=f32, precision=precision)


def _silu(x):
    return x * jax.nn.sigmoid(x)


def _softplus(x):
    return jnp.maximum(x, 0.0) + jnp.log1p(jnp.exp(-jnp.abs(x)))


def _log_sigmoid(x):
    return jnp.minimum(x, 0.0) - jnp.log1p(jnp.exp(-jnp.abs(x)))


def _rms(x, g, eps=1e-6):
    return x * lax.rsqrt(jnp.mean(x * x, axis=-1, keepdims=True) + eps) * g


def _layer_norm(y, g, b, eps=1e-5):
    mu = jnp.mean(y, axis=-1, keepdims=True)
    yc = y - mu
    var = jnp.mean(yc * yc, axis=-1, keepdims=True)
    return yc * lax.rsqrt(var + eps) * g + b


def _modulate(x_ref, ada_ref, sub):
    return x_ref[...] * (1.0 + ada_ref[:, 3 * sub + 1:3 * sub + 2, :]) + ada_ref[:, 3 * sub:3 * sub + 1, :]


def _residual_ln(x_ref, ada_ref, sub, coef, f, g_ref, b_ref):
    gate = ada_ref[:, 3 * sub + 2:3 * sub + 3, :]
    y = ALPHA * x_ref[...] + coef * gate * f.reshape(x_ref.shape)
    return _layer_norm(y, g_ref[...], b_ref[...])


def _ada_kernel(c_ref, w_ref, b_ref, o_ref):
    sc = _silu(c_ref[...]).astype(bf16)
    o_ref[...] = _dot(sc, w_ref[...].astype(bf16)) + b_ref[...]


def _ada_call(c_all, w_ada, b_ada):
    n, d = c_all.shape
    depth, _, wide = w_ada.shape
    tn = 1024
    return pl.pallas_call(
        _ada_kernel,
        out_shape=jax.ShapeDtypeStruct((depth, n, wide), f32),
        grid=(depth, wide // tn),
        in_specs=[
            pl.BlockSpec((n, d), lambda l, j: (0, 0)),
            pl.BlockSpec((None, d, tn), lambda l, j: (l, 0, j)),
            pl.BlockSpec((None, 1, tn), lambda l, j: (l, 0, j)),
        ],
        out_specs=pl.BlockSpec((None, n, tn), lambda l, j: (l, 0, j)),
        compiler_params=_cparams("parallel", "parallel"),
        name="ada_proj",
    )(c_all, w_ada, b_ada.reshape(depth, 1, wide))


def _ffn_kernel(x_ref, ada_ref, w1_ref, w3_ref, w2_ref, g_ref, b_ref, o_ref, *, sub):
    bB, bT, D = x_ref.shape
    h = _modulate(x_ref, ada_ref, sub).reshape(bB * bT, D).astype(bf16)
    a = _dot(h, w1_ref[...])
    b = _dot(h, w3_ref[...])
    f = _dot((_silu(a) * b).astype(bf16), w2_ref[...])
    o_ref[...] = _residual_ln(x_ref, ada_ref, sub, 0.5, f, g_ref, b_ref)


def _ffn_call(x, ada, w1, w3, w2, ln_g, ln_b, layer, half):
    B, T, D = x.shape
    F = w1.shape[-1]
    bB, bT = _row_tiles(B, T, FFN_ROWS)
    sub = 2 * half
    row = lambda b, t: (b, t, 0)
    resident = lambda shape: pl.BlockSpec(shape, lambda b, t: (layer, half, 0, 0), pipeline_mode=pl.Buffered(1))
    ln = pl.BlockSpec((None, None, 1, D), lambda b, t: (layer, sub, 0, 0))
    return pl.pallas_call(
        functools.partial(_ffn_kernel, sub=sub),
        out_shape=jax.ShapeDtypeStruct((B, T, D), f32),
        grid=(B // bB, T // bT),
        in_specs=[
            pl.BlockSpec((bB, bT, D), row),
            pl.BlockSpec((bB, 9, D), lambda b, t: (b, 0, 0)),
            resident((None, None, D, F)), resident((None, None, D, F)), resident((None, None, F, D)), ln, ln,
        ],
        out_specs=pl.BlockSpec((bB, bT, D), row),
        compiler_params=_cparams("parallel", "parallel"),
        name="ffn",
    )(x, ada, w1, w3, w2, ln_g, ln_b)


def _rope_tables(pos, half, width, used):
    inv = ROPE_BASE ** (-jnp.arange(half, dtype=f32) / half)
    ang = pos.astype(f32)[:, None] * inv[None, :]
    cos, sin = jnp.cos(ang), jnp.sin(ang)
    lane = np.arange(width)
    idx = lane % half
    live = lane < used
    first = (lane % (2 * half)) < half
    c = jnp.where(live[None, :], cos[:, idx], 0.0)
    s1 = jnp.where((live & first)[None, :], -sin[:, idx], 0.0)
    s2 = jnp.where((live & ~first)[None, :], sin[:, idx], 0.0)
    return c, s1, s2


def _rope(x, c, s1, s2, half):
    w = x.shape[-1]
    return x * c + pltpu.roll(x, w - half, 1) * s1 + pltpu.roll(x, half, 1) * s2


EVEN_COLS = 2048


def _even_pre_kernel(x_ref, ada_ref, win_ref, qn_ref, wqn_ref, wqr_ref, wuk_ref, kvn_ref,
                     cm_ref, s1m_ref, s2m_ref, cr_ref, s1r_ref, s2r_ref,
                     q_ref, kc_ref, lat_ref, kr_ref, rq_ref, rk_ref, rv_ref, rg_ref):
    bB, bT, D = x_ref.shape
    R = bB * bT
    h = _modulate(x_ref, ada_ref, 1).reshape(R, D).astype(bf16)
    proj = _dot(h, win_ref[...])
    cqn = _rms(proj[:, 0:256], qn_ref[...]).astype(bf16)
    q_nope = _dot(cqn, wqn_ref[...]).astype(bf16)
    q_rope = _dot(cqn, wqr_ref[...])
    lat = _rms(proj[:, 256:384], kvn_ref[...])
    cm, s1m, s2m = cm_ref[...], s1m_ref[...], s2m_ref[...]
    half_m = MLA_ROPE // 2
    kr = _rope(proj[:, 384:512], cm, s1m, s2m, half_m)
    lat_ref[...] = lat.reshape(bB, bT, LANES)
    kr_ref[...] = kr[:, :MLA_ROPE].reshape(bB, bT, MLA_ROPE)
    kr_ones = jnp.where(lax.broadcasted_iota(jnp.int32, (1, LANES), 1) < MLA_ROPE, kr, 1.0)
    kc_ref[...] = jnp.concatenate([lat, kr_ones], axis=1).astype(bf16).reshape(bB, bT, 2 * LANES)
    for p in range(MLA_HEADS // 2):
        q_lat2 = _dot(q_nope[:, LANES * p:LANES * (p + 1)], wuk_ref[p])
        for hh in range(2):
            hd = 2 * p + hh
            qr = _rope(q_rope[:, LANES * hd:LANES * (hd + 1)], cm, s1m, s2m, half_m)
            qc = jnp.concatenate([q_lat2[:, LANES * hh:LANES * (hh + 1)], qr], axis=1)
            q_ref[:, hd] = (qc * MLA_SCALE).astype(bf16).reshape(bB, bT, 2 * LANES)
    cr, s1r, s2r = cr_ref[...], s1r_ref[...], s2r_ref[...]
    half_r = RET_DK // 2
    rq_ref[...] = _rope(proj[:, 512:768], cr, s1r, s2r, half_r).reshape(bB, bT, 256)
    rk_ref[...] = (_rope(proj[:, 768:1024], cr, s1r, s2r, half_r) * RET_DK ** -0.5).reshape(bB, bT, 256)
    rv_ref[...] = proj[:, 1024:1536].astype(bf16).reshape(bB, bT, 512)
    rg_ref[...] = proj[:, 1536:2048].astype(bf16).reshape(bB, bT, 512)


def _even_pre_call(x, ada, wts, tabs):
    B, T, D = x.shape
    bB, bT = _row_tiles(B, T, MIX_ROWS)
    R = bB * bT
    row = lambda b, t: (b, t, 0)
    full = lambda shape: pl.BlockSpec(shape, lambda b, t: (0,) * len(shape))
    tab_map = (lambda b, t: (t, 0)) if bB == 1 else (lambda b, t: (0, 0))
    tab = lambda w: pl.BlockSpec((R, w), tab_map)
    out3 = lambda w, dt: jax.ShapeDtypeStruct((B, T, w), dt)
    return pl.pallas_call(
        _even_pre_kernel,
        out_shape=(
            jax.ShapeDtypeStruct((B, MLA_HEADS, T, 256), bf16),
            out3(256, bf16), out3(MLA_KV_RANK, f32), out3(MLA_ROPE, f32),
            out3(256, f32), out3(256, f32), out3(512, bf16), out3(512, bf16),
        ),
        grid=(B // bB, T // bT),
        in_specs=[
            pl.BlockSpec((bB, bT, D), row),
            pl.BlockSpec((bB, 9, D), lambda b, t: (b, 0, 0)),
            full((D, EVEN_COLS)), full((1, 256)), full((256, 512)), full((256, 1024)),
            full((4, LANES, 256)), full((1, LANES)),
            tab(LANES), tab(LANES), tab(LANES), tab(256), tab(256), tab(256),
        ],
        out_specs=(
            pl.BlockSpec((bB, MLA_HEADS, bT, 256), lambda b, t: (b, 0, t, 0)),
            pl.BlockSpec((bB, bT, 256), row), pl.BlockSpec((bB, bT, MLA_KV_RANK), row),
            pl.BlockSpec((bB, bT, MLA_ROPE), row),
            pl.BlockSpec((bB, bT, 256), row), pl.BlockSpec((bB, bT, 256), row),
            pl.BlockSpec((bB, bT, 512), row), pl.BlockSpec((bB, bT, 512), row),
        ),
        compiler_params=_cparams("parallel", "parallel"),
        name="even_pre",
    )(x, ada, wts["w_in"], wts["q_norm"], wts["w_q_nope"], wts["w_q_rope"], wts["w_uk_bd"], wts["kv_norm"], *tabs)


MLA_SCALE = (MLA_NOPE + MLA_ROPE) ** -0.5


def _mla_prompt_kernel(q_ref, kc_ref, o_ref, m_sc, acc_sc, *, tq):
    qi = pl.program_id(1)
    m_sc[...] = jnp.full_like(m_sc, -jnp.inf)
    acc_sc[...] = jnp.zeros_like(acc_sc)

    def block(j, masked):
        kblk = kc_ref[0, pl.ds(pl.multiple_of(j * tq, tq), tq), :]
        for h in range(MLA_HEADS):
            rows = slice(h * tq, (h + 1) * tq)
            s = _dot_nt(q_ref[0, h], kblk)
            if masked:
                qpos = lax.broadcasted_iota(jnp.int32, (tq, tq), 0)
                kpos = lax.broadcasted_iota(jnp.int32, (tq, tq), 1)
                s = jnp.where(kpos <= qpos, s, -jnp.inf)
            m_prev = m_sc[rows, :]
            m_new = jnp.maximum(m_prev, jnp.broadcast_to(jnp.max(s, axis=-1, keepdims=True), (tq, LANES)))
            alpha = jnp.exp(m_prev - m_new)
            p = jnp.exp(s - jnp.concatenate([m_new] * (tq // LANES), axis=1))
            acc_sc[rows, :] = jnp.concatenate([alpha, alpha], axis=1) * acc_sc[rows, :] + _dot(p.astype(bf16), kblk)
            m_sc[rows, :] = m_new

    def body(j, carry):
        block(j, False)
        return carry

    lax.fori_loop(0, qi, body, 0)
    block(qi, True)
    for h in range(MLA_HEADS):
        acc = acc_sc[h * tq:(h + 1) * tq, :]
        o_ref[0, :, LANES * h:LANES * (h + 1)] = (acc[:, :LANES] / acc[:, 2 * LANES - 1:2 * LANES]).astype(bf16)


def _mla_prompt_call(q, kc):
    B, H, T, W = q.shape
    tq = min(ATT_TQ, T)
    R = H * tq
    return pl.pallas_call(
        functools.partial(_mla_prompt_kernel, tq=tq),
        out_shape=jax.ShapeDtypeStruct((B, T, H * LANES), bf16),
        grid=(B, T // tq),
        in_specs=[
            pl.BlockSpec((1, H, tq, W), lambda b, i: (b, 0, i, 0)),
            pl.BlockSpec((1, T, W), lambda b, i: (b, 0, 0)),
        ],
        out_specs=pl.BlockSpec((1, tq, H * LANES), lambda b, i: (b, i, 0)),
        scratch_shapes=[pltpu.VMEM((R, LANES), f32), pltpu.VMEM((R, W), f32)],
        compiler_params=_cparams("parallel", "parallel"),
        name="mla_prompt",
    )(q, kc)


def _mla_sample_kernel(pt_ref, q_ref, kc_ref, lat_hbm, krt_hbm, o_ref, lat_buf, krt_buf, sem, *, T, layer):
    b = pl.program_id(0)
    n_pages = pt_ref.shape[1]
    R = MLA_HEADS * T
    slot = b % 2

    def page_copies(seq, dst, pg):
        page = pt_ref[seq, pg]
        span = pl.ds(pl.multiple_of(pg * PAGE_SIZE, PAGE_SIZE), PAGE_SIZE)
        return (pltpu.make_async_copy(lat_hbm.at[layer, page], lat_buf.at[dst, span, :], sem.at[dst, 0]),
                pltpu.make_async_copy(krt_hbm.at[layer, page], krt_buf.at[dst, :, span], sem.at[dst, 1]))

    def start_pages(seq, dst):
        def body(pg, carry):
            for cp in page_copies(seq, dst, pg):
                cp.start()
            return carry
        lax.fori_loop(0, n_pages, body, 0, unroll=8)

    @pl.when(b == 0)
    def _():
        start_pages(0, 0)

    @pl.when(b + 1 < pl.num_programs(0))
    def _():
        start_pages(b + 1, 1 - slot)

    pltpu.make_async_copy(lat_buf.at[1 - slot], lat_buf.at[slot], sem.at[slot, 0]).wait()
    pltpu.make_async_copy(krt_buf.at[1 - slot], krt_buf.at[slot], sem.at[slot, 1]).wait()

    q = q_ref[0].reshape(R, 2 * LANES)
    lat = lat_buf[slot].astype(bf16)
    krt = krt_buf[slot].astype(bf16)
    s = _dot_nt(q[:, :LANES], lat) + _dot(q[:, LANES:LANES + MLA_ROPE], krt)
    kc = kc_ref[0]
    qpos = lax.broadcasted_iota(jnp.int32, (R, T), 0) % T
    kpos = lax.broadcasted_iota(jnp.int32, (R, T), 1)
    s_new = jnp.where(kpos <= qpos, _dot_nt(q, kc), -jnp.inf)
    m = jnp.maximum(jnp.max(s, axis=-1, keepdims=True), jnp.max(s_new, axis=-1, keepdims=True))
    p = jnp.exp(s - m)
    p_new = jnp.exp(s_new - m)
    l = jnp.sum(p, axis=-1, keepdims=True) + jnp.sum(p_new, axis=-1, keepdims=True)
    o = ((_dot(p.astype(bf16), lat) + _dot(p_new.astype(bf16), kc[:, :LANES])) / l).astype(bf16)
    for h in range(MLA_HEADS):
        o_ref[0, :, LANES * h:LANES * (h + 1)] = o[h * T:(h + 1) * T]


def _mla_sample_call(q, kc, cache_lat, cache_krt, page_table, layer):
    B, H, T, W = q.shape
    n_keys = page_table.shape[1] * PAGE_SIZE
    grid_spec = pltpu.PrefetchScalarGridSpec(
        num_scalar_prefetch=1,
        grid=(B,),
        in_specs=[
            pl.BlockSpec((1, H, T, W), lambda b, pt: (b, 0, 0, 0)),
            pl.BlockSpec((1, T, W), lambda b, pt: (b, 0, 0)),
            pl.BlockSpec(memory_space=pl.ANY),
            pl.BlockSpec(memory_space=pl.ANY),
        ],
        out_specs=pl.BlockSpec((1, T, H * LANES), lambda b, pt: (b, 0, 0)),
        scratch_shapes=[
            pltpu.VMEM((2, n_keys, MLA_KV_RANK), f32),
            pltpu.VMEM((2, MLA_ROPE, n_keys), f32),
            pltpu.SemaphoreType.DMA((2, 2)),
        ],
    )
    return pl.pallas_call(
        functools.partial(_mla_sample_kernel, T=T, layer=layer),
        out_shape=jax.ShapeDtypeStruct((B, T, H * LANES), bf16),
        grid_spec=grid_spec,
        compiler_params=_cparams("arbitrary"),
        name="mla_sample",
    )(page_table, q, kc, cache_lat, cache_krt)


RET_LOG_GAMMA = tuple(float(np.log1p(-np.exp2(-5.0 - h))) for h in range(RET_HEADS))


def _ret_kernel(q_ref, k_ref, v_ref, s0_ref, o_ref, sf_ref, s_sc, *, nb, L):
    c = pl.program_id(1)
    HK = RET_HEADS * RET_DK

    @pl.when(c == 0)
    def _():
        s_sc[...] = s0_ref[...].reshape(nb, HK, RET_DV)

    ti = lax.broadcasted_iota(jnp.int32, (L, L), 0)
    ji = lax.broadcasted_iota(jnp.int32, (L, L), 1)
    causal = ti >= ji
    dist = (ti - ji).astype(f32)
    tcol = lax.broadcasted_iota(jnp.int32, (L, 1), 0).astype(f32)
    lane_head = lax.broadcasted_iota(jnp.int32, (1, HK), 1) // RET_DK
    row_head = lax.broadcasted_iota(jnp.int32, (HK, 1), 0) // RET_DK
    decay = [jnp.exp(jnp.where(causal, dist * lg, -jnp.inf)) for lg in RET_LOG_GAMMA]
    grow = [jnp.exp((tcol + 1.0) * lg) for lg in RET_LOG_GAMMA]
    tail = [jnp.exp((L - 1.0 - tcol) * lg) for lg in RET_LOG_GAMMA]
    row_decay = jnp.zeros((HK, 1), f32)
    for h, lg in enumerate(RET_LOG_GAMMA):
        row_decay = jnp.where(row_head == h, math.exp(L * lg), row_decay)
    for sq in range(nb):
        q, k, v = q_ref[sq], k_ref[sq], v_ref[sq]
        kb = k.astype(bf16)
        s = s_sc[sq]
        sb = s.astype(bf16)
        upd = jnp.zeros((HK, RET_DV), f32)
        for h in range(RET_HEADS):
            mh = lane_head == h
            qh = jnp.where(mh, q, 0.0)
            p = (_dot_nt(qh.astype(bf16), kb) * decay[h]).astype(bf16)
            vh = v[:, RET_DV * h:RET_DV * (h + 1)].astype(bf16)
            o_ref[sq, :, RET_DV * h:RET_DV * (h + 1)] = _dot(p, vh) + _dot((qh * grow[h]).astype(bf16), sb)
            upd = upd + _dot_tn((jnp.where(mh, k, 0.0) * tail[h]).astype(bf16), vh)
        s_sc[sq] = row_decay * s + upd

    @pl.when(c == pl.num_programs(1) - 1)
    def _():
        sf_ref[...] = s_sc[...].reshape(nb, RET_HEADS, RET_DK, RET_DV)


def _ret_call(rq, rk, rv, s0_all, layer):
    B, T, _ = rq.shape
    L = min(SCAN_CHUNK, T)
    nb = min(_row_tiles(B, T, SCAN_ROWS)[0], SCAN_SEQS)
    HK, HV = RET_HEADS * RET_DK, RET_HEADS * RET_DV
    row = lambda b, c: (b, c, 0)
    st_shape = (RET_HEADS, RET_DK, RET_DV)
    st_in = pl.BlockSpec((None, nb) + st_shape, lambda b, c: (layer, b, 0, 0, 0))
    st = pl.BlockSpec((nb,) + st_shape, lambda b, c: (b, 0, 0, 0))
    return pl.pallas_call(
        functools.partial(_ret_kernel, nb=nb, L=L),
        out_shape=(jax.ShapeDtypeStruct((B, T, HV), f32), jax.ShapeDtypeStruct((B,) + st_shape, f32)),
        grid=(B // nb, T // L),
        in_specs=[pl.BlockSpec((nb, L, HK), row), pl.BlockSpec((nb, L, HK), row), pl.BlockSpec((nb, L, HV), row), st_in],
        out_specs=(pl.BlockSpec((nb, L, HV), row), st),
        scratch_shapes=[pltpu.VMEM((nb, HK, RET_DV), f32)],
        compiler_params=_cparams("parallel", "arbitrary"),
        name="retention",
    )(rq, rk, rv, s0_all)


def _head_rms(x, g, width):
    return jnp.concatenate([_rms(x[:, i:i + width], g) for i in range(0, x.shape[1], width)], axis=1)


def _even_post_kernel(x_ref, ada_ref, ol_ref, ro_ref, rg_ref, wuv_ref, rn_ref, wo_ref, g_ref, b_ref, o_ref):
    bB, bT, D = x_ref.shape
    R = bB * bT
    mla_out = _dot(ol_ref[...].reshape(R, MLA_HEADS * LANES), wuv_ref[...])
    ret = _silu(rg_ref[...].reshape(R, 512).astype(f32)) * _head_rms(ro_ref[...].reshape(R, 512), rn_ref[...], RET_DV)
    cat = jnp.concatenate([mla_out, ret], axis=1).astype(bf16)
    mix = _dot(cat, wo_ref[...])
    o_ref[...] = _residual_ln(x_ref, ada_ref, 1, 1.0, mix, g_ref, b_ref)


def _even_post_call(x, ada, o_lat, ro, rg, wts, ln_g, ln_b, layer):
    B, T, D = x.shape
    bB, bT = _row_tiles(B, T, MIX_ROWS)
    row = lambda b, t: (b, t, 0)
    full = lambda shape: pl.BlockSpec(shape, lambda b, t: (0,) * len(shape))
    ln = pl.BlockSpec((None, None, 1, D), lambda b, t: (layer, 1, 0, 0))
    return pl.pallas_call(
        _even_post_kernel,
        out_shape=jax.ShapeDtypeStruct((B, T, D), f32),
        grid=(B // bB, T // bT),
        in_specs=[
            pl.BlockSpec((bB, bT, D), row), pl.BlockSpec((bB, 9, D), lambda b, t: (b, 0, 0)),
            pl.BlockSpec((bB, bT, MLA_HEADS * LANES), row), pl.BlockSpec((bB, bT, 512), row),
            pl.BlockSpec((bB, bT, 512), row),
            full((MLA_HEADS * LANES, MLA_HEADS * MLA_V)), full((1, RET_DV)), full((1024, D)), ln, ln,
        ],
        out_specs=pl.BlockSpec((bB, bT, D), row),
        compiler_params=_cparams("parallel", "parallel"),
        name="even_post",
    )(x, ada, o_lat, ro, rg, wts["w_uv_bd"], wts["ret_norm"], wts["w_out"], ln_g, ln_b)


ODD_COLS = 4352
ODD_OFF = dict(gq=0, gk=256, gv=512, glr=1024, gr=1152, z=1664, xbc=2688, dt=4224)


def _odd_pre_kernel(x_ref, ada_ref, win_ref, wg_ref, bg_ref, cw_ref, cb_ref, dtb_ref, c0_ref,
                    gq_ref, gk_ref, gv_ref, la_ref, gr_ref, z_ref, xs_ref, bm_ref, cm_ref, dt_ref, cs_ref, buf):
    bB, bT, D = x_ref.shape
    R = bB * bT
    t = pl.program_id(1)
    O = ODD_OFF
    h = _modulate(x_ref, ada_ref, 1).reshape(R, D).astype(bf16)
    proj = _dot(h, win_ref[...])
    sl = lambda name, w: proj[:, O[name]:O[name] + w]
    gq_ref[...] = (sl("gq", 256) * GLA_DK ** -0.5).reshape(bB, bT, 256)
    gk_ref[...] = sl("gk", 256).reshape(bB, bT, 256)
    gv_ref[...] = sl("gv", 512).astype(bf16).reshape(bB, bT, 512)
    gr_ref[...] = sl("gr", 512).astype(bf16).reshape(bB, bT, 512)
    z_ref[...] = sl("z", 1024).astype(bf16).reshape(bB, bT, 1024)
    gate_pre = _dot(sl("glr", LANES).astype(bf16), wg_ref[...]) + bg_ref[...]
    la_ref[...] = (_log_sigmoid(gate_pre) / GLA_TAU).reshape(bB, bT, 256)
    dt_ref[...] = _softplus(sl("dt", SSM_HEADS) + dtb_ref[...]).reshape(bB, bT, SSM_HEADS)

    W = SSM_CONV - 1
    lo = SUBLANES - W

    @pl.when(t == 0)
    def _():
        buf[:, lo:SUBLANES, :] = c0_ref[...]

    xbc = sl("xbc", SSM_CONV_DIM).reshape(bB, bT, SSM_CONV_DIM)
    buf[:, SUBLANES:SUBLANES + bT, :] = xbc
    cw = cw_ref[...]
    conv = cb_ref[...] + sum(buf[:, lo + i:lo + i + bT, :] * cw[i:i + 1, :] for i in range(SSM_CONV))
    tail = buf[:, bT + lo:bT + SUBLANES, :]
    buf[:, lo:SUBLANES, :] = tail

    @pl.when(t == pl.num_programs(1) - 1)
    def _():
        cs_ref[...] = tail

    act = _silu(conv)
    xs_ref[...] = act[:, :, :SSM_INNER].astype(bf16)
    bm_ref[...] = act[:, :, SSM_INNER:SSM_INNER + 256].astype(bf16)
    cm_ref[...] = act[:, :, SSM_INNER + 256:].astype(bf16)


def _odd_pre_call(x, ada, wts, conv0_all, layer):
    B, T, D = x.shape
    bB, bT = _row_tiles(B, T, MIX_ROWS)
    row = lambda b, t: (b, t, 0)
    full = lambda shape: pl.BlockSpec(shape, lambda b, t: (0,) * len(shape))
    blk = lambda w: pl.BlockSpec((bB, bT, w), row)
    widths = (256, 256, 512, 256, 512, 1024, SSM_INNER, 256, 256, SSM_HEADS)
    dtypes = (f32, f32, bf16, f32, bf16, bf16, bf16, bf16, bf16, f32)
    cs = pl.BlockSpec((bB, SSM_CONV - 1, SSM_CONV_DIM), lambda b, t: (b, 0, 0))
    cs_in = pl.BlockSpec((None, bB, SSM_CONV - 1, SSM_CONV_DIM), lambda b, t: (layer, b, 0, 0))
    return pl.pallas_call(
        _odd_pre_kernel,
        out_shape=tuple(jax.ShapeDtypeStruct((B, T, w), dt) for w, dt in zip(widths, dtypes))
        + (jax.ShapeDtypeStruct(conv0_all.shape[1:], f32),),
        grid=(B // bB, T // bT),
        in_specs=[
            blk(D), pl.BlockSpec((bB, 9, D), lambda b, t: (b, 0, 0)),
            full((D, ODD_COLS)), full((LANES, 256)), full((1, 256)),
            full((SSM_CONV, SSM_CONV_DIM)), full((1, SSM_CONV_DIM)), full((1, SSM_HEADS)), cs_in,
        ],
        out_specs=tuple(blk(w) for w in widths) + (cs,),
        scratch_shapes=[pltpu.VMEM((bB, SUBLANES + bT, SSM_CONV_DIM), f32)],
        compiler_params=_cparams("parallel", "arbitrary"),
        name="odd_pre",
    )(x, ada, wts["w_in"], wts["w_gate"], wts["b_gate"], wts["conv_w"], wts["conv_b"], wts["dt_bias"], conv0_all)


def _gla_chunk(q, k, la, v, s, sel, o_ref, sq, r_base, *, L, sb):
    HK, HV = GLA_HEADS * GLA_DK, GLA_HEADS * GLA_DV
    row_head = lax.broadcasted_iota(jnp.int32, (HK, HV), 0) // GLA_DK
    col_head = lax.broadcasted_iota(jnp.int32, (HK, HV), 1) // GLA_DV
    lane_hk = lax.broadcasted_iota(jnp.int32, (1, HK), 1) // GLA_DK
    lane_hv = lax.broadcasted_iota(jnp.int32, (1, HV), 1) // GLA_DV
    vb = v.astype(bf16)
    v = v.astype(f32)
    ti = lax.broadcasted_iota(jnp.int32, (L, L), 0)
    ji = lax.broadcasted_iota(jnp.int32, (L, L), 1)
    seg = _dot((ti >= ji).astype(f32), la, precision=HIGHEST)
    o_inter = _dot((q * jnp.exp(seg)).astype(bf16), s.astype(bf16))
    t3 = lax.broadcasted_iota(jnp.int32, (sb, sb, 1), 0)
    j3 = lax.broadcasted_iota(jnp.int32, (sb, sb, 1), 1)
    for i in range(L // sb):
        r0 = i * sb
        qi, ki, si, vi = q[r0:r0 + sb], k[r0:r0 + sb], seg[r0:r0 + sb], v[r0:r0 + sb]
        diff = si[:, None, :] - si[None, :, :]
        e = qi[:, None, :] * ki[None, :, :] * jnp.exp(jnp.where(t3 >= j3, diff, -jnp.inf))
        sc = _dot(e.reshape(sb * sb, HK).astype(bf16), sel)
        o = o_inter[r0:r0 + sb] + jnp.sum(sc.reshape(sb, sb, HV) * vi[None, :, :], axis=1)
        if i > 0:
            segp = seg[r0 - 1:r0]
            qt = qi * jnp.exp(si - segp)
            kt = k[:r0] * jnp.exp(segp - seg[:r0])
            qs = jnp.concatenate([jnp.where(lane_hk == h, qt, 0.0) for h in range(GLA_HEADS)], axis=0)
            pr = _dot_nt(qs.astype(bf16), kt.astype(bf16)).astype(bf16)
            full = _dot(pr, vb[:r0])
            for h in range(GLA_HEADS):
                o = o + jnp.where(lane_hv == h, full[h * sb:(h + 1) * sb], 0.0)
        o_ref[sq, r_base + r0:r_base + r0 + sb, :] = o
    seg_last = seg[L - 1:L]
    kt = (k * jnp.exp(seg_last - seg)).astype(bf16)
    upd = _dot_tn(kt, vb)
    tot = _dot_tn(la, jnp.ones((L, GLA_DV), f32), precision=HIGHEST)
    row_decay = jnp.concatenate([jnp.exp(tot)] * GLA_HEADS, axis=1)
    return jnp.where(row_head == col_head, row_decay * s + upd, 0.0)


def _gla_kernel(q_ref, k_ref, la_ref, v_ref, s0_ref, sel_ref, o_ref, sf_ref, s_sc, *, nb, nc, L, sb):
    c = pl.program_id(1)

    @pl.when(c == 0)
    def _():
        s_sc[...] = jnp.zeros_like(s_sc)
        for sq in range(nb):
            for h in range(GLA_HEADS):
                s_sc[sq, GLA_DK * h:GLA_DK * (h + 1), GLA_DV * h:GLA_DV * (h + 1)] = s0_ref[sq, h]

    sel = sel_ref[...]
    for sq in range(nb):
        s = s_sc[sq]
        for ci in range(nc):
            rows = slice(ci * L, (ci + 1) * L)
            s = _gla_chunk(q_ref[sq, rows, :], k_ref[sq, rows, :], la_ref[sq, rows, :], v_ref[sq, rows, :], s, sel,
                           o_ref, sq, ci * L, L=L, sb=sb)
        s_sc[sq] = s

    @pl.when(c == pl.num_programs(1) - 1)
    def _():
        for sq in range(nb):
            for h in range(GLA_HEADS):
                sf_ref[sq, h] = s_sc[sq, GLA_DK * h:GLA_DK * (h + 1), GLA_DV * h:GLA_DV * (h + 1)]


def _gla_call(gq, gk, la, gv, s0_all, sel, layer):
    B, T, _ = gq.shape
    L = min(GLA_CHUNK, T)
    sb = min(GLA_SUB, L)
    nb, Lb = _row_tiles(B, T, SCAN_ROWS)
    nb = min(nb, SCAN_SEQS)
    HK, HV = GLA_HEADS * GLA_DK, GLA_HEADS * GLA_DV
    row = lambda b, c: (b, c, 0)
    st_shape = (GLA_HEADS, GLA_DK, GLA_DV)
    st_in = pl.BlockSpec((None, nb) + st_shape, lambda b, c: (layer, b, 0, 0, 0))
    st = pl.BlockSpec((nb,) + st_shape, lambda b, c: (b, 0, 0, 0))
    return pl.pallas_call(
        functools.partial(_gla_kernel, nb=nb, nc=Lb // L, L=L, sb=sb),
        out_shape=(jax.ShapeDtypeStruct((B, T, HV), f32), jax.ShapeDtypeStruct((B,) + st_shape, f32)),
        grid=(B // nb, T // Lb),
        in_specs=[pl.BlockSpec((nb, Lb, HK), row)] * 3 + [pl.BlockSpec((nb, Lb, HV), row), st_in,
                  pl.BlockSpec((HK, HV), lambda b, c: (0, 0))],
        out_specs=(pl.BlockSpec((nb, Lb, HV), row), st),
        scratch_shapes=[pltpu.VMEM((nb, HK, HV), f32)],
        compiler_params=_cparams("parallel", "arbitrary"),
        name="gla",
    )(gq, gk, la, gv, s0_all, sel)


def _ssd_seq(xs_ref, bm_ref, cm_ref, dt_ref, a_row, d_ref, y_ref, s_sc, sq, *, L):
    P, N = SSM_HEADDIM, SSM_DSTATE
    pairs = SSM_HEADS // 2
    dt = dt_ref[sq]
    la = dt * a_row
    ti = lax.broadcasted_iota(jnp.int32, (L, L), 0)
    ji = lax.broadcasted_iota(jnp.int32, (L, L), 1)
    causal = ti >= ji
    seg = _dot(causal.astype(f32), la, precision=HIGHEST)
    seg_t = _dot_tn(la, (ti <= ji).astype(f32), precision=HIGHEST)
    shift_t = seg_t - jnp.log(dt.T)
    seg_last = seg[L - 1:L]
    grow = jnp.exp(seg)
    wtail = dt * jnp.exp(seg_last - seg)
    chunk_decay = jnp.exp(seg_last)
    lo = lax.broadcasted_iota(jnp.int32, (1, 2 * P), 1) < P
    row_lo = lax.broadcasted_iota(jnp.int32, (2 * P, 1), 0) < P
    for g in range(SSM_GROUPS):
        cg = cm_ref[sq, :, N * g:N * (g + 1)].astype(bf16)
        bg = bm_ref[sq, :, N * g:N * (g + 1)].astype(bf16)
        gmat = _dot_nt(cg, bg)
        for i in range(g * pairs // SSM_GROUPS, (g + 1) * pairs // SSM_GROUPS):
            ha, hb = 2 * i, 2 * i + 1
            cols = slice(2 * P * i, 2 * P * (i + 1))
            xb = xs_ref[sq, :, cols]
            xp = xb.astype(f32)
            zero = jnp.zeros_like(xb)
            pa = (gmat * jnp.exp(jnp.where(causal, seg[:, ha:ha + 1] - shift_t[ha:ha + 1, :], -jnp.inf))).astype(bf16)
            pb = (gmat * jnp.exp(jnp.where(causal, seg[:, hb:hb + 1] - shift_t[hb:hb + 1, :], -jnp.inf))).astype(bf16)
            y = _dot(pa, jnp.where(lo, xb, zero)) + _dot(pb, jnp.where(lo, zero, xb))
            sp = s_sc[sq, i]
            y = y + _dot_nt(cg, sp.astype(bf16)) * jnp.where(lo, grow[:, ha:ha + 1], grow[:, hb:hb + 1])
            y_ref[sq, :, cols] = y + d_ref[:, cols] * xp
            xw = (xp * jnp.where(lo, wtail[:, ha:ha + 1], wtail[:, hb:hb + 1])).astype(bf16)
            decay = jnp.where(row_lo, chunk_decay[:, ha:ha + 1], chunk_decay[:, hb:hb + 1])
            s_sc[sq, i] = decay * sp + _dot_tn(xw, bg)


def _ssd_kernel(xs_ref, bm_ref, cm_ref, dt_ref, a_ref, d_ref, s0_ref, y_ref, sf_ref, s_sc, *, nb, L):
    c = pl.program_id(1)
    P, N = SSM_HEADDIM, SSM_DSTATE
    pairs = SSM_HEADS // 2

    @pl.when(c == 0)
    def _():
        s_sc[...] = s0_ref[...].reshape(nb, pairs, 2 * P, N)

    a_row = a_ref[...]
    for sq in range(nb):
        _ssd_seq(xs_ref, bm_ref, cm_ref, dt_ref, a_row, d_ref, y_ref, s_sc, sq, L=L)

    @pl.when(c == pl.num_programs(1) - 1)
    def _():
        sf_ref[...] = s_sc[...].reshape(nb, SSM_HEADS, P, N)


def _ssd_call(xs, bm, cm, dt, a_row, d_row, s0_t, layer):
    B, T, _ = xs.shape
    L = min(SCAN_CHUNK, T)
    nb = min(_row_tiles(B, T, SCAN_ROWS)[0], SSD_SEQS)
    row = lambda b, c: (b, c, 0)
    st_shape = (SSM_HEADS, SSM_HEADDIM, SSM_DSTATE)
    return pl.pallas_call(
        functools.partial(_ssd_kernel, nb=nb, L=L),
        out_shape=(jax.ShapeDtypeStruct((B, T, SSM_INNER), f32), jax.ShapeDtypeStruct((B,) + st_shape, f32)),
        grid=(B // nb, T // L),
        in_specs=[
            pl.BlockSpec((nb, L, SSM_INNER), row), pl.BlockSpec((nb, L, 256), row), pl.BlockSpec((nb, L, 256), row),
            pl.BlockSpec((nb, L, SSM_HEADS), row),
            pl.BlockSpec((1, SSM_HEADS), lambda b, c: (0, 0)), pl.BlockSpec((1, SSM_INNER), lambda b, c: (0, 0)),
            pl.BlockSpec((None, nb) + st_shape, lambda b, c: (layer, b, 0, 0, 0)),
        ],
        out_specs=(pl.BlockSpec((nb, L, SSM_INNER), row), pl.BlockSpec((nb,) + st_shape, lambda b, c: (b, 0, 0, 0))),
        scratch_shapes=[pltpu.VMEM((nb, SSM_HEADS // 2, 2 * SSM_HEADDIM, SSM_DSTATE), f32)],
        compiler_params=_cparams("parallel", "arbitrary"),
        name="ssd",
    )(xs, bm, cm, dt, a_row, d_row, s0_t)


def _odd_post_kernel(x_ref, ada_ref, go_ref, gr_ref, y_ref, z_ref, gn_ref, sn_ref, wo_ref, g_ref, b_ref, o_ref):
    bB, bT, D = x_ref.shape
    R = bB * bT
    gla = _silu(gr_ref[...].reshape(R, 512).astype(f32)) * _head_rms(go_ref[...].reshape(R, 512), gn_ref[...], GLA_DV)
    yz = y_ref[...].reshape(R, SSM_INNER) * _silu(z_ref[...].reshape(R, SSM_INNER).astype(f32))
    gw = SSM_INNER // SSM_GROUPS
    sn = sn_ref[...]
    yn = jnp.concatenate([_rms(yz[:, gw * g:gw * (g + 1)], sn[:, gw * g:gw * (g + 1)]) for g in range(SSM_GROUPS)], axis=1)
    cat = jnp.concatenate([gla, yn], axis=1).astype(bf16)
    mix = _dot(cat, wo_ref[...])
    o_ref[...] = _residual_ln(x_ref, ada_ref, 1, 1.0, mix, g_ref, b_ref)


def _odd_post_call(x, ada, go, gr, y, z, wts, ln_g, ln_b, layer):
    B, T, D = x.shape
    bB, bT = _row_tiles(B, T, MIX_ROWS)
    row = lambda b, t: (b, t, 0)
    full = lambda shape: pl.BlockSpec(shape, lambda b, t: (0,) * len(shape))
    blk = lambda w: pl.BlockSpec((bB, bT, w), row)
    ln = pl.BlockSpec((None, None, 1, D), lambda b, t: (layer, 1, 0, 0))
    return pl.pallas_call(
        _odd_post_kernel,
        out_shape=jax.ShapeDtypeStruct((B, T, D), f32),
        grid=(B // bB, T // bT),
        in_specs=[
            blk(D), pl.BlockSpec((bB, 9, D), lambda b, t: (b, 0, 0)),
            blk(512), blk(512), blk(SSM_INNER), blk(SSM_INNER),
            full((1, GLA_DV)), full((1, SSM_INNER)), full((512 + SSM_INNER, D)), ln, ln,
        ],
        out_specs=blk(D),
        compiler_params=_cparams("parallel", "parallel"),
        name="odd_post",
    )(x, ada, go, gr, y, z, wts["gla_norm"], wts["ssm_norm"], wts["w_out"], ln_g, ln_b)


def _pad_cols(w, width):
    return jnp.pad(w, ((0, 0), (0, width - w.shape[1])))


def _prep_even(j, w_in_even, mla_q_norm, w_uq, mla_kv_norm, w_uk, w_uv, ret_norm, w_out_even):
    w = w_in_even[j]
    k0 = MLA_Q_RANK + MLA_KV_RANK
    w_in = jnp.concatenate([w[:, :k0], _pad_cols(w[:, k0:k0 + MLA_ROPE], LANES), w[:, k0 + MLA_ROPE:]], axis=1)
    uq = w_uq[j].reshape(MLA_Q_RANK, MLA_HEADS, MLA_NOPE + MLA_ROPE)
    w_q_nope = uq[:, :, :MLA_NOPE].reshape(MLA_Q_RANK, MLA_HEADS * MLA_NOPE)
    w_q_rope = jnp.pad(uq[:, :, MLA_NOPE:], ((0, 0), (0, 0), (0, LANES - MLA_ROPE))).reshape(MLA_Q_RANK, MLA_HEADS * LANES)
    eye2 = jnp.eye(2, dtype=f32)
    uk = jnp.transpose(w_uk[j], (1, 2, 0)).reshape(MLA_HEADS // 2, 2, MLA_NOPE, MLA_KV_RANK)
    w_uk_bd = (uk[:, :, :, None, :] * eye2[None, :, None, :, None]).reshape(MLA_HEADS // 2, 2 * MLA_NOPE, 2 * MLA_KV_RANK)
    eye8 = jnp.eye(MLA_HEADS, dtype=f32)
    uv = jnp.transpose(w_uv[j], (1, 0, 2))
    w_uv_bd = (uv[:, :, None, :] * eye8[:, None, :, None]).reshape(MLA_HEADS * MLA_KV_RANK, MLA_HEADS * MLA_V)
    return dict(
        w_in=w_in.astype(bf16), q_norm=mla_q_norm[j][None, :], w_q_nope=w_q_nope.astype(bf16),
        w_q_rope=w_q_rope.astype(bf16), w_uk_bd=w_uk_bd.astype(bf16), kv_norm=mla_kv_norm[j][None, :],
        w_uv_bd=w_uv_bd.astype(bf16), ret_norm=ret_norm[j][None, :], w_out=w_out_even[j].astype(bf16),
    )


def _prep_odd(j, w_in_odd, gla_w_gate, gla_b_gate, gla_norm, ssm_conv_w, ssm_conv_b, ssm_dt_bias, ssm_a_log, ssm_d,
              ssm_norm, w_out_odd):
    w = w_in_odd[j]
    c_glr = 2 * GLA_HEADS * GLA_DK + GLA_HEADS * GLA_DV
    c_dt = w.shape[1] - SSM_HEADS
    w_in = jnp.concatenate([
        w[:, :c_glr], _pad_cols(w[:, c_glr:c_glr + GLA_GATE_RANK], LANES), w[:, c_glr + GLA_GATE_RANK:c_dt],
        _pad_cols(w[:, c_dt:], LANES)], axis=1)
    w_gate = jnp.pad(gla_w_gate[j], ((0, LANES - GLA_GATE_RANK), (0, 0)))
    return dict(
        w_in=w_in.astype(bf16), w_gate=w_gate.astype(bf16), b_gate=gla_b_gate[j][None, :],
        conv_w=ssm_conv_w[j], conv_b=ssm_conv_b[j][None, :], dt_bias=ssm_dt_bias[j][None, :],
        a_row=-jnp.exp(ssm_a_log[j].astype(f32))[None, :], d_row=jnp.repeat(ssm_d[j], SSM_HEADDIM)[None, :],
        gla_norm=gla_norm[j][None, :], ssm_norm=ssm_norm[j][None, :], w_out=w_out_odd[j].astype(bf16),
    )


def _gla_select():
    r = np.arange(GLA_HEADS * GLA_DK)[:, None] // GLA_DK
    c = np.arange(GLA_HEADS * GLA_DV)[None, :] // GLA_DV
    return jnp.asarray(r == c, dtype=bf16)


def _trunk(x, ada_all, pos, p, ret_s0, gla_s0, ssm_s0_t, conv0, cache_lat, cache_krt, page_table):
    B, T, D = x.shape
    bB, _ = _row_tiles(B, T, MIX_ROWS)
    tabs_m = _rope_tables(pos, MLA_ROPE // 2, LANES, MLA_ROPE)
    tabs_r = _rope_tables(pos, RET_DK // 2, RET_HEADS * RET_DK, RET_HEADS * RET_DK)
    if bB > 1:
        tabs_m = tuple(jnp.tile(a, (bB, 1)) for a in tabs_m)
        tabs_r = tuple(jnp.tile(a, (bB, 1)) for a in tabs_r)
    tabs = tabs_m + tabs_r
    sel = _gla_select()
    lat_l, kr_l, ret_l, gla_l, ssm_l, conv_l = [], [], [], [], [], []
    for l in range(DEPTH):
        ada = ada_all[l]
        j = l // 2
        x = _ffn_call(x, ada, p["ffn_w1"], p["ffn_w3"], p["ffn_w2"], p["ln_g"], p["ln_b"], l, 0)
        if l % 2 == 0:
            wts = p["even"][j]
            q, kc, lat, kr, rq, rk, rv, rg = _even_pre_call(x, ada, wts, tabs)
            if page_table is None:
                o_lat = _mla_prompt_call(q, kc)
            else:
                o_lat = _mla_sample_call(q, kc, cache_lat, cache_krt, page_table, j)
            ro, rs = _ret_call(rq, rk, rv, ret_s0, j)
            x = _even_post_call(x, ada, o_lat, ro, rg, wts, p["ln_g"], p["ln_b"], l)
            lat_l.append(lat)
            kr_l.append(kr)
            ret_l.append(rs)
        else:
            wts = p["odd"][j]
            gq, gk, gv, la, gr, z, xs, bm, cm, dt, cs = _odd_pre_call(x, ada, wts, conv0, j)
            go, gs = _gla_call(gq, gk, la, gv, gla_s0, sel, j)
            y, ss = _ssd_call(xs, bm, cm, dt, wts["a_row"], wts["d_row"], ssm_s0_t, j)
            x = _odd_post_call(x, ada, go, gr, y, z, wts, p["ln_g"], p["ln_b"], l)
            gla_l.append(gs)
            ssm_l.append(ss)
            conv_l.append(cs)
        x = _ffn_call(x, ada, p["ffn_w1"], p["ffn_w3"], p["ffn_w2"], p["ln_g"], p["ln_b"], l, 1)
    ssm_s = jnp.swapaxes(jnp.stack(ssm_l), -1, -2)
    return x, jnp.stack(lat_l), jnp.stack(kr_l), jnp.stack(ret_l), jnp.stack(gla_l), ssm_s, jnp.stack(conv_l)


def kernel(x_prompt, x_sample, cache_mla_latent, cache_mla_krope, page_table, state_retention, state_gla, state_ssm, state_conv, c_prompt, c_sample, w_ada, b_ada, ln_g, ln_b, ffn_w1, ffn_w3, ffn_w2, w_in_even, mla_q_norm, w_uq, mla_kv_norm, w_uk, w_uv, ret_norm, w_out_even, w_in_odd, gla_w_gate, gla_b_gate, gla_norm, ssm_conv_w, ssm_conv_b, ssm_dt_bias, ssm_a_log, ssm_d, ssm_norm, w_out_odd):
    n_even, n_odd = (DEPTH + 1) // 2, DEPTH // 2
    bp, tp, D = x_prompt.shape
    bs, ts, _ = x_sample.shape
    p = dict(
        ffn_w1=ffn_w1.astype(bf16), ffn_w3=ffn_w3.astype(bf16), ffn_w2=ffn_w2.astype(bf16),
        ln_g=ln_g.reshape(DEPTH, 3, 1, D), ln_b=ln_b.reshape(DEPTH, 3, 1, D),
        even=[_prep_even(j, w_in_even, mla_q_norm, w_uq, mla_kv_norm, w_uk, w_uv, ret_norm, w_out_even)
              for j in range(n_even)],
        odd=[_prep_odd(j, w_in_odd, gla_w_gate, gla_b_gate, gla_norm, ssm_conv_w, ssm_conv_b, ssm_dt_bias,
                       ssm_a_log, ssm_d, ssm_norm, w_out_odd) for j in range(n_odd)],
    )
    ada_all = _ada_call(jnp.concatenate([c_prompt, c_sample], axis=0), w_ada, b_ada)
    ada_p = ada_all[:, :bp].reshape(DEPTH, bp, 9, D)
    ada_s = ada_all[:, bp:].reshape(DEPTH, bs, 9, D)
    past_len = page_table.shape[1] * PAGE_SIZE
    pos_p = jnp.arange(tp, dtype=jnp.int32)
    pos_s = past_len + jnp.arange(ts, dtype=jnp.int32)
    ret0 = jnp.zeros((n_even, bp) + state_retention.shape[2:], f32)
    gla0 = jnp.zeros((n_odd, bp) + state_gla.shape[2:], f32)
    ssm0_t = jnp.zeros((n_odd, bp, SSM_HEADS, SSM_HEADDIM, SSM_DSTATE), f32)
    conv0 = jnp.zeros((n_odd, bp) + state_conv.shape[2:], f32)
    out_p = _trunk(x_prompt, ada_p, pos_p, p, ret0, gla0, ssm0_t, conv0, None, None, None)
    out_s = _trunk(x_sample, ada_s, pos_s, p, state_retention, state_gla, jnp.swapaxes(state_ssm, -1, -2), state_conv,
                   cache_mla_latent, jnp.swapaxes(cache_mla_krope, -1, -2), page_table)
    return tuple(a for pair in zip(out_p, out_s) for a in pair)
```

```python
import functools
import math

import jax
import jax.numpy as jnp
import numpy as np
from jax import lax
from jax.experimental import pallas as pl
from jax.experimental.pallas import tpu as pltpu

f32 = jnp.float32
bf16 = jnp.bfloat16
HIGHEST = lax.Precision.HIGHEST

DEPTH = 4
ALPHA = (2.0 * DEPTH) ** 0.25
ROPE_BASE = 10000.0
PAGE_SIZE = 128
MLA_HEADS, MLA_Q_RANK, MLA_KV_RANK, MLA_NOPE, MLA_ROPE, MLA_V = 8, 256, 128, 64, 32, 64
RET_HEADS, RET_DK, RET_DV = 4, 64, 128
GLA_HEADS, GLA_DK, GLA_DV, GLA_GATE_RANK, GLA_TAU = 4, 64, 128, 16, 16.0
SSM_HEADS, SSM_HEADDIM, SSM_GROUPS, SSM_DSTATE, SSM_CONV = 16, 64, 2, 128, 4
SSM_INNER = SSM_HEADS * SSM_HEADDIM
SSM_CONV_DIM = SSM_INNER + 2 * SSM_GROUPS * SSM_DSTATE
LANES = 128
SUBLANES = 8

FFN_ROWS = 512
MIX_ROWS = 512
ATT_TQ = 512
SCAN_CHUNK = 256
SCAN_ROWS = 256
SCAN_SEQS = 8
SSD_SEQS = 4
GLA_CHUNK = 64
GLA_SUB = 16
SAMPLE_KEY_SPANS = 1
VMEM_LIMIT = 56 * 1024 * 1024


def _cparams(*sem):
    return pltpu.CompilerParams(dimension_semantics=sem, vmem_limit_bytes=VMEM_LIMIT)


def _row_tiles(B, T, rows):
    if T >= rows:
        return 1, rows
    return min(B, rows // T), T


def _dot(a, b, precision=None):
    return jnp.dot(a, b, preferred_element_type=f32, precision=precision)


def _dot_nt(a, b):
    return lax.dot_general(a, b, (((1,), (1,)), ((), ())), preferred_element_type=f32)


def _dot_tn(a, b, precision=None):
    return lax.dot_general(a, b, (((0,), (0,)), ((), ())), preferred_element_type=f32, precision=precision)


def _silu(x):
    return x * jax.nn.sigmoid(x)


def _softplus(x):
    return jnp.maximum(x, 0.0) + jnp.log1p(jnp.exp(-jnp.abs(x)))


def _log_sigmoid(x):
    return jnp.minimum(x, 0.0) - jnp.log1p(jnp.exp(-jnp.abs(x)))


def _rms(x, g, eps=1e-6):
    return x * lax.rsqrt(jnp.mean(x * x, axis=-1, keepdims=True) + eps) * g


def _layer_norm(y, g, b, eps=1e-5):
    mu = jnp.mean(y, axis=-1, keepdims=True)
    yc = y - mu
    var = jnp.mean(yc * yc, axis=-1, keepdims=True)
    return yc * lax.rsqrt(var + eps) * g + b


def _modulate(x_ref, ada_ref, sub):
    return x_ref[...] * (1.0 + ada_ref[:, 3 * sub + 1:3 * sub + 2, :]) + ada_ref[:, 3 * sub:3 * sub + 1, :]


def _residual_ln(x_ref, ada_ref, sub, coef, f, g_ref, b_ref):
    gate = ada_ref[:, 3 * sub + 2:3 * sub + 3, :]
    y = ALPHA * x_ref[...] + coef * gate * f.reshape(x_ref.shape)
    return _layer_norm(y, g_ref[...], b_ref[...])


def _halves(bB, bT):
    if bB == 1:
        return [(slice(0, 1), slice(k * bT // 2, (k + 1) * bT // 2)) for k in range(2)]
    return [(slice(k * bB // 2, (k + 1) * bB // 2), slice(0, bT)) for k in range(2)]


def _mixer_residual_ln(x_ref, ada_ref, o_ref, bs, ts, mix, g_ref, b_ref):
    x = x_ref[bs, ts, :]
    y = ALPHA * x + ada_ref[bs, 5:6, :] * mix.reshape(x.shape)
    o_ref[bs, ts, :] = _layer_norm(y, g_ref[...], b_ref[...])


def _ada_kernel(c_ref, w_ref, b_ref, o_ref):
    sc = _silu(c_ref[...]).astype(bf16)
    o_ref[...] = _dot(sc, w_ref[...].astype(bf16)) + b_ref[...]


def _ada_call(c_all, w_ada, b_ada):
    n, d = c_all.shape
    depth, _, wide = w_ada.shape
    tn = 1024
    return pl.pallas_call(
        _ada_kernel,
        out_shape=jax.ShapeDtypeStruct((depth, n, wide), f32),
        grid=(depth, wide // tn),
        in_specs=[
            pl.BlockSpec((n, d), lambda l, j: (0, 0)),
            pl.BlockSpec((None, d, tn), lambda l, j: (l, 0, j)),
            pl.BlockSpec((None, 1, tn), lambda l, j: (l, 0, j)),
        ],
        out_specs=pl.BlockSpec((None, n, tn), lambda l, j: (l, 0, j)),
        compiler_params=_cparams("parallel", "parallel"),
        name="ada_proj",
    )(c_all, w_ada, b_ada.reshape(depth, 1, wide))


def _ffn_kernel(x_ref, ada_ref, w1_ref, w3_ref, w2_ref, g_ref, b_ref, o_ref, *, sub):
    bB, bT, D = x_ref.shape
    h = _modulate(x_ref, ada_ref, sub).reshape(bB * bT, D).astype(bf16)
    a = _dot(h, w1_ref[...])
    b = _dot(h, w3_ref[...])
    f = _dot((_silu(a) * b).astype(bf16), w2_ref[...])
    o_ref[...] = _residual_ln(x_ref, ada_ref, sub, 0.5, f, g_ref, b_ref)


def _ffn_call(x, ada, w1, w3, w2, ln_g, ln_b, layer, half):
    B, T, D = x.shape
    F = w1.shape[-1]
    bB, bT = _row_tiles(B, T, FFN_ROWS)
    sub = 2 * half
    row = lambda b, t: (b, t, 0)
    resident = lambda shape: pl.BlockSpec(shape, lambda b, t: (layer, half, 0, 0), pipeline_mode=pl.Buffered(1))
    ln = pl.BlockSpec((None, None, 1, D), lambda b, t: (layer, sub, 0, 0))
    return pl.pallas_call(
        functools.partial(_ffn_kernel, sub=sub),
        out_shape=jax.ShapeDtypeStruct((B, T, D), f32),
        grid=(B // bB, T // bT),
        in_specs=[
            pl.BlockSpec((bB, bT, D), row),
            pl.BlockSpec((bB, 9, D), lambda b, t: (b, 0, 0)),
            resident((None, None, D, F)), resident((None, None, D, F)), resident((None, None, F, D)), ln, ln,
        ],
        out_specs=pl.BlockSpec((bB, bT, D), row),
        compiler_params=_cparams("parallel", "parallel"),
        name="ffn",
    )(x, ada, w1, w3, w2, ln_g, ln_b)


def _rope_tables(pos, half, width, used):
    inv = ROPE_BASE ** (-jnp.arange(half, dtype=f32) / half)
    ang = pos.astype(f32)[:, None] * inv[None, :]
    cos, sin = jnp.cos(ang), jnp.sin(ang)
    lane = np.arange(width)
    idx = lane % half
    live = lane < used
    first = (lane % (2 * half)) < half
    c = jnp.where(live[None, :], cos[:, idx], 0.0)
    s1 = jnp.where((live & first)[None, :], -sin[:, idx], 0.0)
    s2 = jnp.where((live & ~first)[None, :], sin[:, idx], 0.0)
    return c, s1, s2


def _rope(x, c, s1, s2, half):
    w = x.shape[-1]
    return x * c + pltpu.roll(x, w - half, 1) * s1 + pltpu.roll(x, half, 1) * s2


EVEN_COLS = 2048


def _even_pre_kernel(x_ref, ada_ref, win_ref, qn_ref, wqn_ref, wqr_ref, wuk_ref, kvn_ref,
                     cm_ref, s1m_ref, s2m_ref, cr_ref, s1r_ref, s2r_ref,
                     q_ref, kc_ref, lat_ref, kr_ref, rq_ref, rk_ref, rv_ref, rg_ref):
    bB, bT, D = x_ref.shape
    R = bB * bT
    h = _modulate(x_ref, ada_ref, 1).reshape(R, D).astype(bf16)
    proj = _dot(h, win_ref[...])
    cqn = _rms(proj[:, 0:256], qn_ref[...]).astype(bf16)
    q_nope = _dot(cqn, wqn_ref[...]).astype(bf16)
    q_rope = _dot(cqn, wqr_ref[...])
    lat = _rms(proj[:, 256:384], kvn_ref[...])
    cm, s1m, s2m = cm_ref[...], s1m_ref[...], s2m_ref[...]
    half_m = MLA_ROPE // 2
    kr = _rope(proj[:, 384:512], cm, s1m, s2m, half_m)
    lat_ref[...] = lat.reshape(bB, bT, LANES)
    kr_ref[...] = kr[:, :MLA_ROPE].reshape(bB, bT, MLA_ROPE)
    kr_ones = jnp.where(lax.broadcasted_iota(jnp.int32, (1, LANES), 1) < MLA_ROPE, kr, 1.0)
    kc_ref[...] = jnp.concatenate([lat, kr_ones], axis=1).astype(bf16).reshape(bB, bT, 2 * LANES)
    for p in range(MLA_HEADS // 2):
        q_lat2 = _dot(q_nope[:, LANES * p:LANES * (p + 1)], wuk_ref[p])
        for hh in range(2):
            hd = 2 * p + hh
            qr = _rope(q_rope[:, LANES * hd:LANES * (hd + 1)], cm, s1m, s2m, half_m)
            qc = jnp.concatenate([q_lat2[:, LANES * hh:LANES * (hh + 1)], qr], axis=1)
            q_ref[:, hd] = (qc * MLA_SCALE).astype(bf16).reshape(bB, bT, 2 * LANES)
    cr, s1r, s2r = cr_ref[...], s1r_ref[...], s2r_ref[...]
    half_r = RET_DK // 2
    rq_ref[...] = _rope(proj[:, 512:768], cr, s1r, s2r, half_r).reshape(bB, bT, 256)
    rk_ref[...] = (_rope(proj[:, 768:1024], cr, s1r, s2r, half_r) * RET_DK ** -0.5).reshape(bB, bT, 256)
    rv_ref[...] = proj[:, 1024:1536].astype(bf16).reshape(bB, bT, 512)
    rg_ref[...] = proj[:, 1536:2048].astype(bf16).reshape(bB, bT, 512)


def _even_pre_call(x, ada, wts, tabs):
    B, T, D = x.shape
    bB, bT = _row_tiles(B, T, MIX_ROWS)
    R = bB * bT
    row = lambda b, t: (b, t, 0)
    full = lambda shape: pl.BlockSpec(shape, lambda b, t: (0,) * len(shape))
    tab_map = (lambda b, t: (t, 0)) if bB == 1 else (lambda b, t: (0, 0))
    tab = lambda w: pl.BlockSpec((R, w), tab_map)
    out3 = lambda w, dt: jax.ShapeDtypeStruct((B, T, w), dt)
    return pl.pallas_call(
        _even_pre_kernel,
        out_shape=(
            jax.ShapeDtypeStruct((B, MLA_HEADS, T, 256), bf16),
            out3(256, bf16), out3(MLA_KV_RANK, f32), out3(MLA_ROPE, f32),
            out3(256, f32), out3(256, f32), out3(512, bf16), out3(512, bf16),
        ),
        grid=(B // bB, T // bT),
        in_specs=[
            pl.BlockSpec((bB, bT, D), row),
            pl.BlockSpec((bB, 9, D), lambda b, t: (b, 0, 0)),
            full((D, EVEN_COLS)), full((1, 256)), full((256, 512)), full((256, 1024)),
            full((4, LANES, 256)), full((1, LANES)),
            tab(LANES), tab(LANES), tab(LANES), tab(256), tab(256), tab(256),
        ],
        out_specs=(
            pl.BlockSpec((bB, MLA_HEADS, bT, 256), lambda b, t: (b, 0, t, 0)),
            pl.BlockSpec((bB, bT, 256), row), pl.BlockSpec((bB, bT, MLA_KV_RANK), row),
            pl.BlockSpec((bB, bT, MLA_ROPE), row),
            pl.BlockSpec((bB, bT, 256), row), pl.BlockSpec((bB, bT, 256), row),
            pl.BlockSpec((bB, bT, 512), row), pl.BlockSpec((bB, bT, 512), row),
        ),
        compiler_params=_cparams("parallel", "parallel"),
        name="even_pre",
    )(x, ada, wts["w_in"], wts["q_norm"], wts["w_q_nope"], wts["w_q_rope"], wts["w_uk_bd"], wts["kv_norm"], *tabs)


MLA_SCALE = (MLA_NOPE + MLA_ROPE) ** -0.5


def _mla_prompt_kernel(q_ref, kc_ref, o_ref, m_sc, acc_sc, *, tq):
    qi = pl.program_id(1)
    m_sc[...] = jnp.full_like(m_sc, -jnp.inf)
    acc_sc[...] = jnp.zeros_like(acc_sc)

    def block(j, masked):
        kblk = kc_ref[0, pl.ds(pl.multiple_of(j * tq, tq), tq), :]
        for h in range(MLA_HEADS):
            rows = slice(h * tq, (h + 1) * tq)
            s = _dot_nt(q_ref[0, h], kblk)
            if masked:
                qpos = lax.broadcasted_iota(jnp.int32, (tq, tq), 0)
                kpos = lax.broadcasted_iota(jnp.int32, (tq, tq), 1)
                s = jnp.where(kpos <= qpos, s, -jnp.inf)
            m_prev = m_sc[rows, :]
            m_new = jnp.maximum(m_prev, jnp.broadcast_to(jnp.max(s, axis=-1, keepdims=True), (tq, LANES)))
            alpha = jnp.exp(m_prev - m_new)
            p = jnp.exp(s - jnp.concatenate([m_new] * (tq // LANES), axis=1))
            acc_sc[rows, :] = jnp.concatenate([alpha, alpha], axis=1) * acc_sc[rows, :] + _dot(p.astype(bf16), kblk)
            m_sc[rows, :] = m_new

    def body(j, carry):
        block(j, False)
        return carry

    lax.fori_loop(0, qi, body, 0)
    block(qi, True)
    for h in range(MLA_HEADS):
        acc = acc_sc[h * tq:(h + 1) * tq, :]
        o_ref[0, :, LANES * h:LANES * (h + 1)] = (acc[:, :LANES] / acc[:, 2 * LANES - 1:2 * LANES]).astype(bf16)


def _mla_prompt_call(q, kc):
    B, H, T, W = q.shape
    tq = min(ATT_TQ, T)
    R = H * tq
    return pl.pallas_call(
        functools.partial(_mla_prompt_kernel, tq=tq),
        out_shape=jax.ShapeDtypeStruct((B, T, H * LANES), bf16),
        grid=(B, T // tq),
        in_specs=[
            pl.BlockSpec((1, H, tq, W), lambda b, i: (b, 0, i, 0)),
            pl.BlockSpec((1, T, W), lambda b, i: (b, 0, 0)),
        ],
        out_specs=pl.BlockSpec((1, tq, H * LANES), lambda b, i: (b, i, 0)),
        scratch_shapes=[pltpu.VMEM((R, LANES), f32), pltpu.VMEM((R, W), f32)],
        compiler_params=_cparams("parallel", "parallel"),
        name="mla_prompt",
    )(q, kc)


def _mla_sample_kernel(pt_ref, q_ref, kc_ref, lat_hbm, krt_hbm, o_ref, lat_buf, krt_buf, sem, *, T, layer):
    b = pl.program_id(0)
    n_pages = pt_ref.shape[1]
    R = MLA_HEADS * T
    slot = b % 2

    def page_copies(seq, dst, pg):
        page = pt_ref[seq, pg]
        span = pl.ds(pl.multiple_of(pg * PAGE_SIZE, PAGE_SIZE), PAGE_SIZE)
        return (pltpu.make_async_copy(lat_hbm.at[layer, page], lat_buf.at[dst, span, :], sem.at[dst, 0]),
                pltpu.make_async_copy(krt_hbm.at[layer, page], krt_buf.at[dst, :, span], sem.at[dst, 1]))

    def start_pages(seq, dst):
        def body(pg, carry):
            for cp in page_copies(seq, dst, pg):
                cp.start()
            return carry
        lax.fori_loop(0, n_pages, body, 0, unroll=8)

    @pl.when(b == 0)
    def _():
        start_pages(0, 0)

    @pl.when(b + 1 < pl.num_programs(0))
    def _():
        start_pages(b + 1, 1 - slot)

    pltpu.make_async_copy(lat_buf.at[1 - slot], lat_buf.at[slot], sem.at[slot, 0]).wait()
    pltpu.make_async_copy(krt_buf.at[1 - slot], krt_buf.at[slot], sem.at[slot, 1]).wait()

    q = q_ref[0].reshape(R, 2 * LANES)
    kc = kc_ref[0]
    qpos = lax.broadcasted_iota(jnp.int32, (R, T), 0) % T
    kpos = lax.broadcasted_iota(jnp.int32, (R, T), 1)
    s_new = jnp.where(kpos <= qpos, _dot_nt(q, kc), -jnp.inf)
    span = n_pages * PAGE_SIZE // SAMPLE_KEY_SPANS
    parts = []
    for i in range(SAMPLE_KEY_SPANS):
        lat = lat_buf[slot, i * span:(i + 1) * span, :].astype(bf16)
        krt = krt_buf[slot, :, i * span:(i + 1) * span].astype(bf16)
        s = _dot_nt(q[:, :LANES], lat) + _dot(q[:, LANES:LANES + MLA_ROPE], krt)
        m = jnp.max(s, axis=-1, keepdims=True)
        p = jnp.exp(s - m)
        parts.append((m, jnp.sum(p, axis=-1, keepdims=True), _dot(p.astype(bf16), lat)))
    m_new = jnp.max(s_new, axis=-1, keepdims=True)
    p_new = jnp.exp(s_new - m_new)
    parts.append((m_new, jnp.sum(p_new, axis=-1, keepdims=True), _dot(p_new.astype(bf16), kc[:, :LANES])))
    m = functools.reduce(jnp.maximum, [pm for pm, _, _ in parts])
    l = sum(jnp.exp(pm - m) * pl_ for pm, pl_, _ in parts)
    o = (sum(jnp.exp(pm - m) * po for pm, _, po in parts) / l).astype(bf16)
    for h in range(MLA_HEADS):
        o_ref[0, :, LANES * h:LANES * (h + 1)] = o[h * T:(h + 1) * T]


def _mla_sample_call(q, kc, cache_lat, cache_krt, page_table, layer):
    B, H, T, W = q.shape
    n_keys = page_table.shape[1] * PAGE_SIZE
    grid_spec = pltpu.PrefetchScalarGridSpec(
        num_scalar_prefetch=1,
        grid=(B,),
        in_specs=[
            pl.BlockSpec((1, H, T, W), lambda b, pt: (b, 0, 0, 0)),
            pl.BlockSpec((1, T, W), lambda b, pt: (b, 0, 0)),
            pl.BlockSpec(memory_space=pl.ANY),
            pl.BlockSpec(memory_space=pl.ANY),
        ],
        out_specs=pl.BlockSpec((1, T, H * LANES), lambda b, pt: (b, 0, 0)),
        scratch_shapes=[
            pltpu.VMEM((2, n_keys, MLA_KV_RANK), f32),
            pltpu.VMEM((2, MLA_ROPE, n_keys), f32),
            pltpu.SemaphoreType.DMA((2, 2)),
        ],
    )
    return pl.pallas_call(
        functools.partial(_mla_sample_kernel, T=T, layer=layer),
        out_shape=jax.ShapeDtypeStruct((B, T, H * LANES), bf16),
        grid_spec=grid_spec,
        compiler_params=_cparams("arbitrary"),
        name="mla_sample",
    )(page_table, q, kc, cache_lat, cache_krt)


RET_LOG_GAMMA = tuple(float(np.log1p(-np.exp2(-5.0 - h))) for h in range(RET_HEADS))


def _ret_kernel(q_ref, k_ref, v_ref, s0_ref, o_ref, sf_ref, s_sc, *, nb, L):
    c = pl.program_id(1)
    HK = RET_HEADS * RET_DK

    @pl.when(c == 0)
    def _():
        s_sc[...] = s0_ref[...].reshape(nb, HK, RET_DV)

    ti = lax.broadcasted_iota(jnp.int32, (L, L), 0)
    ji = lax.broadcasted_iota(jnp.int32, (L, L), 1)
    causal = ti >= ji
    dist = (ti - ji).astype(f32)
    tcol = lax.broadcasted_iota(jnp.int32, (L, 1), 0).astype(f32)
    lane_head = lax.broadcasted_iota(jnp.int32, (1, HK), 1) // RET_DK
    row_head = lax.broadcasted_iota(jnp.int32, (HK, 1), 0) // RET_DK
    decay = [jnp.exp(jnp.where(causal, dist * lg, -jnp.inf)) for lg in RET_LOG_GAMMA]
    grow = [jnp.exp((tcol + 1.0) * lg) for lg in RET_LOG_GAMMA]
    tail = [jnp.exp((L - 1.0 - tcol) * lg) for lg in RET_LOG_GAMMA]
    row_decay = jnp.zeros((HK, 1), f32)
    for h, lg in enumerate(RET_LOG_GAMMA):
        row_decay = jnp.where(row_head == h, math.exp(L * lg), row_decay)
    for sq in range(nb):
        q, k, v = q_ref[sq], k_ref[sq], v_ref[sq]
        kb = k.astype(bf16)
        s = s_sc[sq]
        sb = s.astype(bf16)
        upd = jnp.zeros((HK, RET_DV), f32)
        for h in range(RET_HEADS):
            mh = lane_head == h
            qh = jnp.where(mh, q, 0.0)
            p = (_dot_nt(qh.astype(bf16), kb) * decay[h]).astype(bf16)
            vh = v[:, RET_DV * h:RET_DV * (h + 1)].astype(bf16)
            o_ref[sq, :, RET_DV * h:RET_DV * (h + 1)] = _dot(p, vh) + _dot((qh * grow[h]).astype(bf16), sb)
            upd = upd + _dot_tn((jnp.where(mh, k, 0.0) * tail[h]).astype(bf16), vh)
        s_sc[sq] = row_decay * s + upd

    @pl.when(c == pl.num_programs(1) - 1)
    def _():
        sf_ref[...] = s_sc[...].reshape(nb, RET_HEADS, RET_DK, RET_DV)


def _ret_call(rq, rk, rv, s0_all, layer):
    B, T, _ = rq.shape
    L = min(SCAN_CHUNK, T)
    nb = min(_row_tiles(B, T, SCAN_ROWS)[0], SCAN_SEQS)
    HK, HV = RET_HEADS * RET_DK, RET_HEADS * RET_DV
    row = lambda b, c: (b, c, 0)
    st_shape = (RET_HEADS, RET_DK, RET_DV)
    st_in = pl.BlockSpec((None, nb) + st_shape, lambda b, c: (layer, b, 0, 0, 0))
    st = pl.BlockSpec((nb,) + st_shape, lambda b, c: (b, 0, 0, 0))
    return pl.pallas_call(
        functools.partial(_ret_kernel, nb=nb, L=L),
        out_shape=(jax.ShapeDtypeStruct((B, T, HV), f32), jax.ShapeDtypeStruct((B,) + st_shape, f32)),
        grid=(B // nb, T // L),
        in_specs=[pl.BlockSpec((nb, L, HK), row), pl.BlockSpec((nb, L, HK), row), pl.BlockSpec((nb, L, HV), row), st_in],
        out_specs=(pl.BlockSpec((nb, L, HV), row), st),
        scratch_shapes=[pltpu.VMEM((nb, HK, RET_DV), f32)],
        compiler_params=_cparams("parallel", "arbitrary"),
        name="retention",
    )(rq, rk, rv, s0_all)


def _head_rms(x, g, width):
    return jnp.concatenate([_rms(x[:, i:i + width], g) for i in range(0, x.shape[1], width)], axis=1)


def _even_post_kernel(x_ref, ada_ref, ol_ref, ro_ref, rg_ref, wuv_ref, rn_ref, wo_ref, g_ref, b_ref, o_ref):
    for bs, ts in _halves(*x_ref.shape[:2]):
        R = (bs.stop - bs.start) * (ts.stop - ts.start)
        rd = lambda ref: ref[bs, ts, :].reshape(R, ref.shape[2])
        mla_out = _dot(rd(ol_ref), wuv_ref[...])
        ret = _silu(rd(rg_ref).astype(f32)) * _head_rms(rd(ro_ref), rn_ref[...], RET_DV)
        cat = jnp.concatenate([mla_out, ret], axis=1).astype(bf16)
        _mixer_residual_ln(x_ref, ada_ref, o_ref, bs, ts, _dot(cat, wo_ref[...]), g_ref, b_ref)


def _even_post_call(x, ada, o_lat, ro, rg, wts, ln_g, ln_b, layer):
    B, T, D = x.shape
    bB, bT = _row_tiles(B, T, MIX_ROWS)
    row = lambda b, t: (b, t, 0)
    full = lambda shape: pl.BlockSpec(shape, lambda b, t: (0,) * len(shape))
    ln = pl.BlockSpec((None, None, 1, D), lambda b, t: (layer, 1, 0, 0))
    return pl.pallas_call(
        _even_post_kernel,
        out_shape=jax.ShapeDtypeStruct((B, T, D), f32),
        grid=(B // bB, T // bT),
        in_specs=[
            pl.BlockSpec((bB, bT, D), row), pl.BlockSpec((bB, 9, D), lambda b, t: (b, 0, 0)),
            pl.BlockSpec((bB, bT, MLA_HEADS * LANES), row), pl.BlockSpec((bB, bT, 512), row),
            pl.BlockSpec((bB, bT, 512), row),
            full((MLA_HEADS * LANES, MLA_HEADS * MLA_V)), full((1, RET_DV)), full((1024, D)), ln, ln,
        ],
        out_specs=pl.BlockSpec((bB, bT, D), row),
        compiler_params=_cparams("parallel", "parallel"),
        name="even_post",
    )(x, ada, o_lat, ro, rg, wts["w_uv_bd"], wts["ret_norm"], wts["w_out"], ln_g, ln_b)


ODD_COLS = 4352
ODD_OFF = dict(gq=0, gk=256, gv=512, glr=1024, gr=1152, z=1664, xbc=2688, dt=4224)


def _odd_pre_kernel(x_ref, ada_ref, win_ref, wg_ref, bg_ref, cw_ref, cb_ref, dtb_ref, c0_ref,
                    gq_ref, gk_ref, gv_ref, la_ref, gr_ref, z_ref, xs_ref, bm_ref, cm_ref, dt_ref, cs_ref, buf):
    bB, bT, D = x_ref.shape
    t = pl.program_id(1)
    O = ODD_OFF
    W = SSM_CONV - 1
    lo = SUBLANES - W

    @pl.when(t == 0)
    def _():
        buf[:, lo:SUBLANES, :] = c0_ref[...]

    cw = cw_ref[...]
    for bs, ts in _halves(bB, bT):
        pb, pt = bs.stop - bs.start, ts.stop - ts.start
        R = pb * pt
        x = x_ref[bs, ts, :]
        h = (x * (1.0 + ada_ref[bs, 4:5, :]) + ada_ref[bs, 3:4, :]).reshape(R, D).astype(bf16)
        proj = _dot(h, win_ref[...])
        sl = lambda name, w: proj[:, O[name]:O[name] + w]
        gq_ref[bs, ts, :] = (sl("gq", 256) * GLA_DK ** -0.5).reshape(pb, pt, 256)
        gk_ref[bs, ts, :] = sl("gk", 256).reshape(pb, pt, 256)
        gv_ref[bs, ts, :] = sl("gv", 512).astype(bf16).reshape(pb, pt, 512)
        gr_ref[bs, ts, :] = sl("gr", 512).astype(bf16).reshape(pb, pt, 512)
        z_ref[bs, ts, :] = sl("z", 1024).astype(bf16).reshape(pb, pt, 1024)
        gate_pre = _dot(sl("glr", LANES).astype(bf16), wg_ref[...]) + bg_ref[...]
        la_ref[bs, ts, :] = (_log_sigmoid(gate_pre) / GLA_TAU).reshape(pb, pt, 256)
        dt_ref[bs, ts, :] = _softplus(sl("dt", SSM_HEADS) + dtb_ref[...]).reshape(pb, pt, SSM_HEADS)
        buf[bs, SUBLANES + ts.start:SUBLANES + ts.stop, :] = sl("xbc", SSM_CONV_DIM).reshape(pb, pt, SSM_CONV_DIM)
        conv = cb_ref[...] + sum(buf[bs, lo + i + ts.start:lo + i + ts.stop, :] * cw[i:i + 1, :]
                                 for i in range(SSM_CONV))
        act = _silu(conv)
        xs_ref[bs, ts, :] = act[:, :, :SSM_INNER].astype(bf16)
        bm_ref[bs, ts, :] = act[:, :, SSM_INNER:SSM_INNER + 256].astype(bf16)
        cm_ref[bs, ts, :] = act[:, :, SSM_INNER + 256:].astype(bf16)
    tail = buf[:, bT + lo:bT + SUBLANES, :]
    buf[:, lo:SUBLANES, :] = tail

    @pl.when(t == pl.num_programs(1) - 1)
    def _():
        cs_ref[...] = tail


def _odd_pre_call(x, ada, wts, conv0_all, layer):
    B, T, D = x.shape
    bB, bT = _row_tiles(B, T, MIX_ROWS)
    row = lambda b, t: (b, t, 0)
    full = lambda shape: pl.BlockSpec(shape, lambda b, t: (0,) * len(shape))
    blk = lambda w: pl.BlockSpec((bB, bT, w), row)
    widths = (256, 256, 512, 256, 512, 1024, SSM_INNER, 256, 256, SSM_HEADS)
    dtypes = (f32, f32, bf16, f32, bf16, bf16, bf16, bf16, bf16, f32)
    cs = pl.BlockSpec((bB, SSM_CONV - 1, SSM_CONV_DIM), lambda b, t: (b, 0, 0))
    cs_in = pl.BlockSpec((None, bB, SSM_CONV - 1, SSM_CONV_DIM), lambda b, t: (layer, b, 0, 0))
    return pl.pallas_call(
        _odd_pre_kernel,
        out_shape=tuple(jax.ShapeDtypeStruct((B, T, w), dt) for w, dt in zip(widths, dtypes))
        + (jax.ShapeDtypeStruct(conv0_all.shape[1:], f32),),
        grid=(B // bB, T // bT),
        in_specs=[
            blk(D), pl.BlockSpec((bB, 9, D), lambda b, t: (b, 0, 0)),
            full((D, ODD_COLS)), full((LANES, 256)), full((1, 256)),
            full((SSM_CONV, SSM_CONV_DIM)), full((1, SSM_CONV_DIM)), full((1, SSM_HEADS)), cs_in,
        ],
        out_specs=tuple(blk(w) for w in widths) + (cs,),
        scratch_shapes=[pltpu.VMEM((bB, SUBLANES + bT, SSM_CONV_DIM), f32)],
        compiler_params=_cparams("parallel", "arbitrary"),
        name="odd_pre",
    )(x, ada, wts["w_in"], wts["w_gate"], wts["b_gate"], wts["conv_w"], wts["conv_b"], wts["dt_bias"], conv0_all)


def _gla_chunk(q, k, la, v, s, sel, o_ref, sq, r_base, *, L, sb):
    HK, HV = GLA_HEADS * GLA_DK, GLA_HEADS * GLA_DV
    row_head = lax.broadcasted_iota(jnp.int32, (HK, HV), 0) // GLA_DK
    col_head = lax.broadcasted_iota(jnp.int32, (HK, HV), 1) // GLA_DV
    lane_hk = lax.broadcasted_iota(jnp.int32, (1, HK), 1) // GLA_DK
    lane_hv = lax.broadcasted_iota(jnp.int32, (1, HV), 1) // GLA_DV
    vb = v.astype(bf16)
    v = v.astype(f32)
    ti = lax.broadcasted_iota(jnp.int32, (L, L), 0)
    ji = lax.broadcasted_iota(jnp.int32, (L, L), 1)
    seg = _dot((ti >= ji).astype(f32), la, precision=HIGHEST)
    o_inter = _dot((q * jnp.exp(seg)).astype(bf16), s.astype(bf16))
    j3 = lax.broadcasted_iota(jnp.int32, (sb, sb, 1), 0)
    t3 = lax.broadcasted_iota(jnp.int32, (sb, sb, 1), 1)
    for i in range(L // sb):
        r0 = i * sb
        qi, ki, si, vi = q[r0:r0 + sb], k[r0:r0 + sb], seg[r0:r0 + sb], v[r0:r0 + sb]
        diff = si[None, :, :] - si[:, None, :]
        e = qi[None, :, :] * ki[:, None, :] * jnp.exp(jnp.where(t3 >= j3, diff, -jnp.inf))
        sc = _dot(e.reshape(sb * sb, HK).astype(bf16), sel)
        o = o_inter[r0:r0 + sb] + jnp.sum(sc.reshape(sb, sb, HV) * vi[:, None, :], axis=0)
        if i > 0:
            segp = seg[r0 - 1:r0]
            qt = qi * jnp.exp(si - segp)
            kt = k[:r0] * jnp.exp(segp - seg[:r0])
            qs = jnp.concatenate([jnp.where(lane_hk == h, qt, 0.0) for h in range(GLA_HEADS)], axis=0)
            pr = _dot_nt(qs.astype(bf16), kt.astype(bf16)).astype(bf16)
            full = _dot(pr, vb[:r0])
            for h in range(GLA_HEADS):
                o = o + jnp.where(lane_hv == h, full[h * sb:(h + 1) * sb], 0.0)
        o_ref[sq, r_base + r0:r_base + r0 + sb, :] = o
    seg_last = seg[L - 1:L]
    kt = (k * jnp.exp(seg_last - seg)).astype(bf16)
    upd = _dot_tn(kt, vb)
    decay_col = jnp.transpose(jnp.broadcast_to(jnp.exp(seg_last), (SUBLANES, HK)))[:, 0:1]
    return jnp.where(row_head == col_head, decay_col * s + upd, 0.0)


def _gla_kernel(q_ref, k_ref, la_ref, v_ref, s0_ref, sel_ref, o_ref, sf_ref, s_sc, *, nb, nc, L, sb):
    c = pl.program_id(1)

    @pl.when(c == 0)
    def _():
        s_sc[...] = jnp.zeros_like(s_sc)
        for sq in range(nb):
            for h in range(GLA_HEADS):
                s_sc[sq, GLA_DK * h:GLA_DK * (h + 1), GLA_DV * h:GLA_DV * (h + 1)] = s0_ref[sq, h]

    sel = sel_ref[...]
    for sq in range(nb):
        s = s_sc[sq]
        for ci in range(nc):
            rows = slice(ci * L, (ci + 1) * L)
            s = _gla_chunk(q_ref[sq, rows, :], k_ref[sq, rows, :], la_ref[sq, rows, :], v_ref[sq, rows, :], s, sel,
                           o_ref, sq, ci * L, L=L, sb=sb)
        s_sc[sq] = s

    @pl.when(c == pl.num_programs(1) - 1)
    def _():
        for sq in range(nb):
            for h in range(GLA_HEADS):
                sf_ref[sq, h] = s_sc[sq, GLA_DK * h:GLA_DK * (h + 1), GLA_DV * h:GLA_DV * (h + 1)]


def _gla_call(gq, gk, la, gv, s0_all, sel, layer):
    B, T, _ = gq.shape
    L = min(GLA_CHUNK, T)
    sb = min(GLA_SUB, L)
    nb, Lb = _row_tiles(B, T, SCAN_ROWS)
    nb = min(nb, SCAN_SEQS)
    HK, HV = GLA_HEADS * GLA_DK, GLA_HEADS * GLA_DV
    row = lambda b, c: (b, c, 0)
    st_shape = (GLA_HEADS, GLA_DK, GLA_DV)
    st_in = pl.BlockSpec((None, nb) + st_shape, lambda b, c: (layer, b, 0, 0, 0))
    st = pl.BlockSpec((nb,) + st_shape, lambda b, c: (b, 0, 0, 0))
    return pl.pallas_call(
        functools.partial(_gla_kernel, nb=nb, nc=Lb // L, L=L, sb=sb),
        out_shape=(jax.ShapeDtypeStruct((B, T, HV), f32), jax.ShapeDtypeStruct((B,) + st_shape, f32)),
        grid=(B // nb, T // Lb),
        in_specs=[pl.BlockSpec((nb, Lb, HK), row)] * 3 + [pl.BlockSpec((nb, Lb, HV), row), st_in,
                  pl.BlockSpec((HK, HV), lambda b, c: (0, 0))],
        out_specs=(pl.BlockSpec((nb, Lb, HV), row), st),
        scratch_shapes=[pltpu.VMEM((nb, HK, HV), f32)],
        compiler_params=_cparams("parallel", "arbitrary"),
        name="gla",
    )(gq, gk, la, gv, s0_all, sel)


def _ssd_seq(xs_ref, bm_ref, cm_ref, dt_ref, a_row, d_ref, y_ref, s_sc, sq, *, L):
    P, N = SSM_HEADDIM, SSM_DSTATE
    pairs = SSM_HEADS // 2
    dt = dt_ref[sq]
    la = dt * a_row
    ti = lax.broadcasted_iota(jnp.int32, (L, L), 0)
    ji = lax.broadcasted_iota(jnp.int32, (L, L), 1)
    causal = ti >= ji
    seg = _dot(causal.astype(f32), la, precision=HIGHEST)
    seg_t = _dot_tn(la, (ti <= ji).astype(f32), precision=HIGHEST)
    shift_t = seg_t - jnp.log(dt.T)
    seg_last = seg[L - 1:L]
    grow = jnp.exp(seg)
    wtail = dt * jnp.exp(seg_last - seg)
    chunk_decay = jnp.exp(seg_last)
    lo = lax.broadcasted_iota(jnp.int32, (1, 2 * P), 1) < P
    row_lo = lax.broadcasted_iota(jnp.int32, (2 * P, 1), 0) < P
    for g in range(SSM_GROUPS):
        cg = cm_ref[sq, :, N * g:N * (g + 1)].astype(bf16)
        bg = bm_ref[sq, :, N * g:N * (g + 1)].astype(bf16)
        gmat = _dot_nt(cg, bg)
        for i in range(g * pairs // SSM_GROUPS, (g + 1) * pairs // SSM_GROUPS):
            ha, hb = 2 * i, 2 * i + 1
            cols = slice(2 * P * i, 2 * P * (i + 1))
            xb = xs_ref[sq, :, cols]
            xp = xb.astype(f32)
            zero = jnp.zeros_like(xb)
            pa = (gmat * jnp.exp(jnp.where(causal, seg[:, ha:ha + 1] - shift_t[ha:ha + 1, :], -jnp.inf))).astype(bf16)
            pb = (gmat * jnp.exp(jnp.where(causal, seg[:, hb:hb + 1] - shift_t[hb:hb + 1, :], -jnp.inf))).astype(bf16)
            y = _dot(pa, jnp.where(lo, xb, zero)) + _dot(pb, jnp.where(lo, zero, xb))
            sp = s_sc[sq, i]
            y = y + _dot_nt(cg, sp.astype(bf16)) * jnp.where(lo, grow[:, ha:ha + 1], grow[:, hb:hb + 1])
            y_ref[sq, :, cols] = y + d_ref[:, cols] * xp
            xw = (xp * jnp.where(lo, wtail[:, ha:ha + 1], wtail[:, hb:hb + 1])).astype(bf16)
            decay = jnp.where(row_lo, chunk_decay[:, ha:ha + 1], chunk_decay[:, hb:hb + 1])
            s_sc[sq, i] = decay * sp + _dot_tn(xw, bg)


def _ssd_kernel(xs_ref, bm_ref, cm_ref, dt_ref, a_ref, d_ref, s0_ref, y_ref, sf_ref, s_sc, *, nb, L):
    c = pl.program_id(1)
    P, N = SSM_HEADDIM, SSM_DSTATE
    pairs = SSM_HEADS // 2

    @pl.when(c == 0)
    def _():
        s_sc[...] = s0_ref[...].reshape(nb, pairs, 2 * P, N)

    a_row = a_ref[...]
    for sq in range(nb):
        _ssd_seq(xs_ref, bm_ref, cm_ref, dt_ref, a_row, d_ref, y_ref, s_sc, sq, L=L)

    @pl.when(c == pl.num_programs(1) - 1)
    def _():
        sf_ref[...] = s_sc[...].reshape(nb, SSM_HEADS, P, N)


def _ssd_call(xs, bm, cm, dt, a_row, d_row, s0_t, layer):
    B, T, _ = xs.shape
    L = min(SCAN_CHUNK, T)
    nb = min(_row_tiles(B, T, SCAN_ROWS)[0], SSD_SEQS)
    row = lambda b, c: (b, c, 0)
    st_shape = (SSM_HEADS, SSM_HEADDIM, SSM_DSTATE)
    return pl.pallas_call(
        functools.partial(_ssd_kernel, nb=nb, L=L),
        out_shape=(jax.ShapeDtypeStruct((B, T, SSM_INNER), f32), jax.ShapeDtypeStruct((B,) + st_shape, f32)),
        grid=(B // nb, T // L),
        in_specs=[
            pl.BlockSpec((nb, L, SSM_INNER), row), pl.BlockSpec((nb, L, 256), row), pl.BlockSpec((nb, L, 256), row),
            pl.BlockSpec((nb, L, SSM_HEADS), row),
            pl.BlockSpec((1, SSM_HEADS), lambda b, c: (0, 0)), pl.BlockSpec((1, SSM_INNER), lambda b, c: (0, 0)),
            pl.BlockSpec((None, nb) + st_shape, lambda b, c: (layer, b, 0, 0, 0)),
        ],
        out_specs=(pl.BlockSpec((nb, L, SSM_INNER), row), pl.BlockSpec((nb,) + st_shape, lambda b, c: (b, 0, 0, 0))),
        scratch_shapes=[pltpu.VMEM((nb, SSM_HEADS // 2, 2 * SSM_HEADDIM, SSM_DSTATE), f32)],
        compiler_params=_cparams("parallel", "arbitrary"),
        name="ssd",
    )(xs, bm, cm, dt, a_row, d_row, s0_t)


def _odd_post_kernel(x_ref, ada_ref, go_ref, gr_ref, y_ref, z_ref, gn_ref, sn_ref, wo_ref, g_ref, b_ref, o_ref):
    gw = SSM_INNER // SSM_GROUPS
    sn = sn_ref[...]
    for bs, ts in _halves(*x_ref.shape[:2]):
        R = (bs.stop - bs.start) * (ts.stop - ts.start)
        rd = lambda ref: ref[bs, ts, :].reshape(R, ref.shape[2])
        gla = _silu(rd(gr_ref).astype(f32)) * _head_rms(rd(go_ref), gn_ref[...], GLA_DV)
        yz = rd(y_ref) * _silu(rd(z_ref).astype(f32))
        yn = jnp.concatenate([_rms(yz[:, gw * g:gw * (g + 1)], sn[:, gw * g:gw * (g + 1)])
                              for g in range(SSM_GROUPS)], axis=1)
        cat = jnp.concatenate([gla, yn], axis=1).astype(bf16)
        _mixer_residual_ln(x_ref, ada_ref, o_ref, bs, ts, _dot(cat, wo_ref[...]), g_ref, b_ref)


def _odd_post_call(x, ada, go, gr, y, z, wts, ln_g, ln_b, layer):
    B, T, D = x.shape
    bB, bT = _row_tiles(B, T, MIX_ROWS)
    row = lambda b, t: (b, t, 0)
    full = lambda shape: pl.BlockSpec(shape, lambda b, t: (0,) * len(shape))
    blk = lambda w: pl.BlockSpec((bB, bT, w), row)
    ln = pl.BlockSpec((None, None, 1, D), lambda b, t: (layer, 1, 0, 0))
    return pl.pallas_call(
        _odd_post_kernel,
        out_shape=jax.ShapeDtypeStruct((B, T, D), f32),
        grid=(B // bB, T // bT),
        in_specs=[
            blk(D), pl.BlockSpec((bB, 9, D), lambda b, t: (b, 0, 0)),
            blk(512), blk(512), blk(SSM_INNER), blk(SSM_INNER),
            full((1, GLA_DV)), full((1, SSM_INNER)), full((512 + SSM_INNER, D)), ln, ln,
        ],
        out_specs=blk(D),
        compiler_params=_cparams("parallel", "parallel"),
        name="odd_post",
    )(x, ada, go, gr, y, z, wts["gla_norm"], wts["ssm_norm"], wts["w_out"], ln_g, ln_b)


def _pad_cols(w, width):
    return jnp.pad(w, ((0, 0), (0, width - w.shape[1])))


def _prep_even(j, w_in_even, mla_q_norm, w_uq, mla_kv_norm, w_uk, w_uv, ret_norm, w_out_even):
    w = w_in_even[j]
    k0 = MLA_Q_RANK + MLA_KV_RANK
    w_in = jnp.concatenate([w[:, :k0], _pad_cols(w[:, k0:k0 + MLA_ROPE], LANES), w[:, k0 + MLA_ROPE:]], axis=1)
    uq = w_uq[j].reshape(MLA_Q_RANK, MLA_HEADS, MLA_NOPE + MLA_ROPE)
    w_q_nope = uq[:, :, :MLA_NOPE].reshape(MLA_Q_RANK, MLA_HEADS * MLA_NOPE)
    w_q_rope = jnp.pad(uq[:, :, MLA_NOPE:], ((0, 0), (0, 0), (0, LANES - MLA_ROPE))).reshape(MLA_Q_RANK, MLA_HEADS * LANES)
    eye2 = jnp.eye(2, dtype=f32)
    uk = jnp.transpose(w_uk[j], (1, 2, 0)).reshape(MLA_HEADS // 2, 2, MLA_NOPE, MLA_KV_RANK)
    w_uk_bd = (uk[:, :, :, None, :] * eye2[None, :, None, :, None]).reshape(MLA_HEADS // 2, 2 * MLA_NOPE, 2 * MLA_KV_RANK)
    eye8 = jnp.eye(MLA_HEADS, dtype=f32)
    uv = jnp.transpose(w_uv[j], (1, 0, 2))
    w_uv_bd = (uv[:, :, None, :] * eye8[:, None, :, None]).reshape(MLA_HEADS * MLA_KV_RANK, MLA_HEADS * MLA_V)
    return dict(
        w_in=w_in.astype(bf16), q_norm=mla_q_norm[j][None, :], w_q_nope=w_q_nope.astype(bf16),
        w_q_rope=w_q_rope.astype(bf16), w_uk_bd=w_uk_bd.astype(bf16), kv_norm=mla_kv_norm[j][None, :],
        w_uv_bd=w_uv_bd.astype(bf16), ret_norm=ret_norm[j][None, :], w_out=w_out_even[j].astype(bf16),
    )


def _prep_odd(j, w_in_odd, gla_w_gate, gla_b_gate, gla_norm, ssm_conv_w, ssm_conv_b, ssm_dt_bias, ssm_a_log, ssm_d,
              ssm_norm, w_out_odd):
    w = w_in_odd[j]
    c_glr = 2 * GLA_HEADS * GLA_DK + GLA_HEADS * GLA_DV
    c_dt = w.shape[1] - SSM_HEADS
    w_in = jnp.concatenate([
        w[:, :c_glr], _pad_cols(w[:, c_glr:c_glr + GLA_GATE_RANK], LANES), w[:, c_glr + GLA_GATE_RANK:c_dt],
        _pad_cols(w[:, c_dt:], LANES)], axis=1)
    w_gate = jnp.pad(gla_w_gate[j], ((0, LANES - GLA_GATE_RANK), (0, 0)))
    return dict(
        w_in=w_in.astype(bf16), w_gate=w_gate.astype(bf16), b_gate=gla_b_gate[j][None, :],
        conv_w=ssm_conv_w[j], conv_b=ssm_conv_b[j][None, :], dt_bias=ssm_dt_bias[j][None, :],
        a_row=-jnp.exp(ssm_a_log[j].astype(f32))[None, :], d_row=jnp.repeat(ssm_d[j], SSM_HEADDIM)[None, :],
        gla_norm=gla_norm[j][None, :], ssm_norm=ssm_norm[j][None, :], w_out=w_out_odd[j].astype(bf16),
    )


def _gla_select():
    r = np.arange(GLA_HEADS * GLA_DK)[:, None] // GLA_DK
    c = np.arange(GLA_HEADS * GLA_DV)[None, :] // GLA_DV
    return jnp.asarray(r == c, dtype=bf16)


def _trunk(x, ada_all, pos, p, ret_s0, gla_s0, ssm_s0_t, conv0, cache_lat, cache_krt, page_table):
    B, T, D = x.shape
    bB, _ = _row_tiles(B, T, MIX_ROWS)
    tabs_m = _rope_tables(pos, MLA_ROPE // 2, LANES, MLA_ROPE)
    tabs_r = _rope_tables(pos, RET_DK // 2, RET_HEADS * RET_DK, RET_HEADS * RET_DK)
    if bB > 1:
        tabs_m = tuple(jnp.tile(a, (bB, 1)) for a in tabs_m)
        tabs_r = tuple(jnp.tile(a, (bB, 1)) for a in tabs_r)
    tabs = tabs_m + tabs_r
    sel = _gla_select()
    lat_l, kr_l, ret_l, gla_l, ssm_l, conv_l = [], [], [], [], [], []
    for l in range(DEPTH):
        ada = ada_all[l]
        j = l // 2
        x = _ffn_call(x, ada, p["ffn_w1"], p["ffn_w3"], p["ffn_w2"], p["ln_g"], p["ln_b"], l, 0)
        if l % 2 == 0:
            wts = p["even"][j]
            q, kc, lat, kr, rq, rk, rv, rg = _even_pre_call(x, ada, wts, tabs)
            if page_table is None:
                o_lat = _mla_prompt_call(q, kc)
            else:
                o_lat = _mla_sample_call(q, kc, cache_lat, cache_krt, page_table, j)
            ro, rs = _ret_call(rq, rk, rv, ret_s0, j)
            x = _even_post_call(x, ada, o_lat, ro, rg, wts, p["ln_g"], p["ln_b"], l)
            lat_l.append(lat)
            kr_l.append(kr)
            ret_l.append(rs)
        else:
            wts = p["odd"][j]
            gq, gk, gv, la, gr, z, xs, bm, cm, dt, cs = _odd_pre_call(x, ada, wts, conv0, j)
            go, gs = _gla_call(gq, gk, la, gv, gla_s0, sel, j)
            y, ss = _ssd_call(xs, bm, cm, dt, wts["a_row"], wts["d_row"], ssm_s0_t, j)
            x = _odd_post_call(x, ada, go, gr, y, z, wts, p["ln_g"], p["ln_b"], l)
            gla_l.append(gs)
            ssm_l.append(ss)
            conv_l.append(cs)
        x = _ffn_call(x, ada, p["ffn_w1"], p["ffn_w3"], p["ffn_w2"], p["ln_g"], p["ln_b"], l, 1)
    ssm_s = jnp.swapaxes(jnp.stack(ssm_l), -1, -2)
    return x, jnp.stack(lat_l), jnp.stack(kr_l), jnp.stack(ret_l), jnp.stack(gla_l), ssm_s, jnp.stack(conv_l)


def kernel(x_prompt, x_sample, cache_mla_latent, cache_mla_krope, page_table, state_retention, state_gla, state_ssm, state_conv, c_prompt, c_sample, w_ada, b_ada, ln_g, ln_b, ffn_w1, ffn_w3, ffn_w2, w_in_even, mla_q_norm, w_uq, mla_kv_norm, w_uk, w_uv, ret_norm, w_out_even, w_in_odd, gla_w_gate, gla_b_gate, gla_norm, ssm_conv_w, ssm_conv_b, ssm_dt_bias, ssm_a_log, ssm_d, ssm_norm, w_out_odd):
    n_even, n_odd = (DEPTH + 1) // 2, DEPTH // 2
    bp, tp, D = x_prompt.shape
    bs, ts, _ = x_sample.shape
    p = dict(
        ffn_w1=ffn_w1.astype(bf16), ffn_w3=ffn_w3.astype(bf16), ffn_w2=ffn_w2.astype(bf16),
        ln_g=ln_g.reshape(DEPTH, 3, 1, D), ln_b=ln_b.reshape(DEPTH, 3, 1, D),
        even=[_prep_even(j, w_in_even, mla_q_norm, w_uq, mla_kv_norm, w_uk, w_uv, ret_norm, w_out_even)
              for j in range(n_even)],
        odd=[_prep_odd(j, w_in_odd, gla_w_gate, gla_b_gate, gla_norm, ssm_conv_w, ssm_conv_b, ssm_dt_bias,
                       ssm_a_log, ssm_d, ssm_norm, w_out_odd) for j in range(n_odd)],
    )
    ada_all = _ada_call(jnp.concatenate([c_prompt, c_sample], axis=0), w_ada, b_ada)
    ada_p = ada_all[:, :bp].reshape(DEPTH, bp, 9, D)
    ada_s = ada_all[:, bp:].reshape(DEPTH, bs, 9, D)
    past_len = page_table.shape[1] * PAGE_SIZE
    pos_p = jnp.arange(tp, dtype=jnp.int32)
    pos_s = past_len + jnp.arange(ts, dtype=jnp.int32)
    ret0 = jnp.zeros((n_even, bp) + state_retention.shape[2:], f32)
    gla0 = jnp.zeros((n_odd, bp) + state_gla.shape[2:], f32)
    ssm0_t = jnp.zeros((n_odd, bp, SSM_HEADS, SSM_HEADDIM, SSM_DSTATE), f32)
    conv0 = jnp.zeros((n_odd, bp) + state_conv.shape[2:], f32)
    out_p = _trunk(x_prompt, ada_p, pos_p, p, ret0, gla0, ssm0_t, conv0, None, None, None)
    out_s = _trunk(x_sample, ada_s, pos_s, p, state_retention, state_gla, jnp.swapaxes(state_ssm, -1, -2), state_conv,
                   cache_mla_latent, jnp.swapaxes(cache_mla_krope, -1, -2), page_table)
    return tuple(a for pair in zip(out_p, out_s) for a in pair)
```

```python
import functools
import math

import jax
import jax.numpy as jnp
import numpy as np
from jax import lax
from jax.experimental import pallas as pl
from jax.experimental.pallas import tpu as pltpu

f32 = jnp.float32
bf16 = jnp.bfloat16
HIGHEST = lax.Precision.HIGHEST

DEPTH = 4
ALPHA = (2.0 * DEPTH) ** 0.25
ROPE_BASE = 10000.0
PAGE_SIZE = 128
MLA_HEADS, MLA_Q_RANK, MLA_KV_RANK, MLA_NOPE, MLA_ROPE, MLA_V = 8, 256, 128, 64, 32, 64
RET_HEADS, RET_DK, RET_DV = 4, 64, 128
GLA_HEADS, GLA_DK, GLA_DV, GLA_GATE_RANK, GLA_TAU = 4, 64, 128, 16, 16.0
SSM_HEADS, SSM_HEADDIM, SSM_GROUPS, SSM_DSTATE, SSM_CONV = 16, 64, 2, 128, 4
SSM_INNER = SSM_HEADS * SSM_HEADDIM
SSM_CONV_DIM = SSM_INNER + 2 * SSM_GROUPS * SSM_DSTATE
LANES = 128
SUBLANES = 8

FFN_ROWS = 512
MIX_ROWS = 512
ATT_TQ = 512
SCAN_CHUNK = 256
SCAN_ROWS = 256
SCAN_SEQS = 8
SSD_SEQS = 4
GLA_CHUNK = 64
GLA_SUB = 16
SAMPLE_KEY_SPANS = 1
VMEM_LIMIT = 56 * 1024 * 1024


def _cparams(*sem):
    return pltpu.CompilerParams(dimension_semantics=sem, vmem_limit_bytes=VMEM_LIMIT)


def _row_tiles(B, T, rows):
    if T >= rows:
        return 1, rows
    return min(B, rows // T), T


def _dot(a, b, precision=None):
    return jnp.dot(a, b, preferred_element_type=f32, precision=precision)


def _dot_nt(a, b):
    return lax.dot_general(a, b, (((1,), (1,)), ((), ())), preferred_element_type=f32)


def _dot_tn(a, b, precision=None):
    return lax.dot_general(a, b, (((0,), (0,)), ((), ())), preferred_element_type=f32, precision=precision)


def _silu(x):
    return x * jax.nn.sigmoid(x)


def _softplus(x):
    return jnp.maximum(x, 0.0) + jnp.log1p(jnp.exp(-jnp.abs(x)))


def _log_sigmoid(x):
    return jnp.minimum(x, 0.0) - jnp.log1p(jnp.exp(-jnp.abs(x)))


def _rms(x, g, eps=1e-6):
    return x * lax.rsqrt(jnp.mean(x * x, axis=-1, keepdims=True) + eps) * g


def _layer_norm(y, g, b, eps=1e-5):
    mu = jnp.mean(y, axis=-1, keepdims=True)
    yc = y - mu
    var = jnp.mean(yc * yc, axis=-1, keepdims=True)
    return yc * lax.rsqrt(var + eps) * g + b


def _modulate(x_ref, ada_ref, sub):
    return x_ref[...] * (1.0 + ada_ref[:, 3 * sub + 1:3 * sub + 2, :]) + ada_ref[:, 3 * sub:3 * sub + 1, :]


def _residual_ln(x_ref, ada_ref, sub, coef, f, g_ref, b_ref):
    gate = ada_ref[:, 3 * sub + 2:3 * sub + 3, :]
    y = ALPHA * x_ref[...] + coef * gate * f.reshape(x_ref.shape)
    return _layer_norm(y, g_ref[...], b_ref[...])


def _halves(bB, bT):
    if bB == 1:
        return [(slice(0, 1), slice(k * bT // 2, (k + 1) * bT // 2)) for k in range(2)]
    return [(slice(k * bB // 2, (k + 1) * bB // 2), slice(0, bT)) for k in range(2)]


def _mixer_residual_ln(x_ref, ada_ref, o_ref, bs, ts, mix, g_ref, b_ref):
    x = x_ref[bs, ts, :]
    y = ALPHA * x + ada_ref[bs, 5:6, :] * mix.reshape(x.shape)
    o_ref[bs, ts, :] = _layer_norm(y, g_ref[...], b_ref[...])


def _ada_kernel(c_ref, w_ref, b_ref, o_ref):
    sc = _silu(c_ref[...]).astype(bf16)
    o_ref[...] = _dot(sc, w_ref[...].astype(bf16)) + b_ref[...]


def _ada_call(c_all, w_ada, b_ada):
    n, d = c_all.shape
    depth, _, wide = w_ada.shape
    tn = 1024
    return pl.pallas_call(
        _ada_kernel,
        out_shape=jax.ShapeDtypeStruct((depth, n, wide), f32),
        grid=(depth, wide // tn),
        in_specs=[
            pl.BlockSpec((n, d), lambda l, j: (0, 0)),
            pl.BlockSpec((None, d, tn), lambda l, j: (l, 0, j)),
            pl.BlockSpec((None, 1, tn), lambda l, j: (l, 0, j)),
        ],
        out_specs=pl.BlockSpec((None, n, tn), lambda l, j: (l, 0, j)),
        compiler_params=_cparams("parallel", "parallel"),
        name="ada_proj",
    )(c_all, w_ada, b_ada.reshape(depth, 1, wide))


def _ffn_kernel(x_ref, ada_ref, w1_ref, w3_ref, w2_ref, g_ref, b_ref, o_ref, *, sub):
    bB, bT, D = x_ref.shape
    h = _modulate(x_ref, ada_ref, sub).reshape(bB * bT, D).astype(bf16)
    a = _dot(h, w1_ref[...])
    b = _dot(h, w3_ref[...])
    f = _dot((_silu(a) * b).astype(bf16), w2_ref[...])
    o_ref[...] = _residual_ln(x_ref, ada_ref, sub, 0.5, f, g_ref, b_ref)


def _ffn_call(x, ada, w1, w3, w2, ln_g, ln_b, layer, half):
    B, T, D = x.shape
    F = w1.shape[-1]
    bB, bT = _row_tiles(B, T, FFN_ROWS)
    sub = 2 * half
    row = lambda b, t: (b, t, 0)
    resident = lambda shape: pl.BlockSpec(shape, lambda b, t: (layer, half, 0, 0), pipeline_mode=pl.Buffered(1))
    ln = pl.BlockSpec((None, None, 1, D), lambda b, t: (layer, sub, 0, 0))
    return pl.pallas_call(
        functools.partial(_ffn_kernel, sub=sub),
        out_shape=jax.ShapeDtypeStruct((B, T, D), f32),
        grid=(B // bB, T // bT),
        in_specs=[
            pl.BlockSpec((bB, bT, D), row),
            pl.BlockSpec((bB, 9, D), lambda b, t: (b, 0, 0)),
            resident((None, None, D, F)), resident((None, None, D, F)), resident((None, None, F, D)), ln, ln,
        ],
        out_specs=pl.BlockSpec((bB, bT, D), row),
        compiler_params=_cparams("parallel", "parallel"),
        name="ffn",
    )(x, ada, w1, w3, w2, ln_g, ln_b)


def _rope_tables(pos, half, width, used):
    inv = ROPE_BASE ** (-jnp.arange(half, dtype=f32) / half)
    ang = pos.astype(f32)[:, None] * inv[None, :]
    cos, sin = jnp.cos(ang), jnp.sin(ang)
    lane = np.arange(width)
    idx = lane % half
    live = lane < used
    first = (lane % (2 * half)) < half
    c = jnp.where(live[None, :], cos[:, idx], 0.0)
    s1 = jnp.where((live & first)[None, :], -sin[:, idx], 0.0)
    s2 = jnp.where((live & ~first)[None, :], sin[:, idx], 0.0)
    return c, s1, s2


def _rope(x, c, s1, s2, half):
    w = x.shape[-1]
    return x * c + pltpu.roll(x, w - half, 1) * s1 + pltpu.roll(x, half, 1) * s2


EVEN_COLS = 2048


def _even_pre_kernel(x_ref, ada_ref, win_ref, qn_ref, wqn_ref, wqr_ref, wuk_ref, kvn_ref,
                     cm_ref, s1m_ref, s2m_ref, cr_ref, s1r_ref, s2r_ref,
                     q_ref, kc_ref, lat_ref, kr_ref, rq_ref, rk_ref, rv_ref, rg_ref):
    bB, bT, D = x_ref.shape
    R = bB * bT
    h = _modulate(x_ref, ada_ref, 1).reshape(R, D).astype(bf16)
    proj = _dot(h, win_ref[...])
    cqn = _rms(proj[:, 0:256], qn_ref[...]).astype(bf16)
    q_nope = _dot(cqn, wqn_ref[...]).astype(bf16)
    q_rope = _dot(cqn, wqr_ref[...])
    lat = _rms(proj[:, 256:384], kvn_ref[...])
    cm, s1m, s2m = cm_ref[...], s1m_ref[...], s2m_ref[...]
    half_m = MLA_ROPE // 2
    kr = _rope(proj[:, 384:512], cm, s1m, s2m, half_m)
    lat_ref[...] = lat.reshape(bB, bT, LANES)
    kr_ref[...] = kr[:, :MLA_ROPE].reshape(bB, bT, MLA_ROPE)
    kr_ones = jnp.where(lax.broadcasted_iota(jnp.int32, (1, LANES), 1) < MLA_ROPE, kr, 1.0)
    kc_ref[...] = jnp.concatenate([lat, kr_ones], axis=1).astype(bf16).reshape(bB, bT, 2 * LANES)
    for p in range(MLA_HEADS // 2):
        q_lat2 = _dot(q_nope[:, LANES * p:LANES * (p + 1)], wuk_ref[p])
        for hh in range(2):
            hd = 2 * p + hh
            qr = _rope(q_rope[:, LANES * hd:LANES * (hd + 1)], cm, s1m, s2m, half_m)
            qc = jnp.concatenate([q_lat2[:, LANES * hh:LANES * (hh + 1)], qr], axis=1)
            q_ref[:, hd] = (qc * MLA_SCALE).astype(bf16).reshape(bB, bT, 2 * LANES)
    cr, s1r, s2r = cr_ref[...], s1r_ref[...], s2r_ref[...]
    half_r = RET_DK // 2
    rq_ref[...] = _rope(proj[:, 512:768], cr, s1r, s2r, half_r).reshape(bB, bT, 256)
    rk_ref[...] = (_rope(proj[:, 768:1024], cr, s1r, s2r, half_r) * RET_DK ** -0.5).reshape(bB, bT, 256)
    rv_ref[...] = proj[:, 1024:1536].astype(bf16).reshape(bB, bT, 512)
    rg_ref[...] = proj[:, 1536:2048].astype(bf16).reshape(bB, bT, 512)


def _even_pre_call(x, ada, wts, tabs):
    B, T, D = x.shape
    bB, bT = _row_tiles(B, T, MIX_ROWS)
    R = bB * bT
    row = lambda b, t: (b, t, 0)
    full = lambda shape: pl.BlockSpec(shape, lambda b, t: (0,) * len(shape))
    tab_map = (lambda b, t: (t, 0)) if bB == 1 else (lambda b, t: (0, 0))
    tab = lambda w: pl.BlockSpec((R, w), tab_map)
    out3 = lambda w, dt: jax.ShapeDtypeStruct((B, T, w), dt)
    return pl.pallas_call(
        _even_pre_kernel,
        out_shape=(
            jax.ShapeDtypeStruct((B, MLA_HEADS, T, 256), bf16),
            out3(256, bf16), out3(MLA_KV_RANK, f32), out3(MLA_ROPE, f32),
            out3(256, f32), out3(256, f32), out3(512, bf16), out3(512, bf16),
        ),
        grid=(B // bB, T // bT),
        in_specs=[
            pl.BlockSpec((bB, bT, D), row),
            pl.BlockSpec((bB, 9, D), lambda b, t: (b, 0, 0)),
            full((D, EVEN_COLS)), full((1, 256)), full((256, 512)), full((256, 1024)),
            full((4, LANES, 256)), full((1, LANES)),
            tab(LANES), tab(LANES), tab(LANES), tab(256), tab(256), tab(256),
        ],
        out_specs=(
            pl.BlockSpec((bB, MLA_HEADS, bT, 256), lambda b, t: (b, 0, t, 0)),
            pl.BlockSpec((bB, bT, 256), row), pl.BlockSpec((bB, bT, MLA_KV_RANK), row),
            pl.BlockSpec((bB, bT, MLA_ROPE), row),
            pl.BlockSpec((bB, bT, 256), row), pl.BlockSpec((bB, bT, 256), row),
            pl.BlockSpec((bB, bT, 512), row), pl.BlockSpec((bB, bT, 512), row),
        ),
        compiler_params=_cparams("parallel", "parallel"),
        name="even_pre",
    )(x, ada, wts["w_in"], wts["q_norm"], wts["w_q_nope"], wts["w_q_rope"], wts["w_uk_bd"], wts["kv_norm"], *tabs)


MLA_SCALE = (MLA_NOPE + MLA_ROPE) ** -0.5


def _mla_prompt_kernel(q_ref, kc_ref, o_ref, m_sc, acc_sc, *, tq):
    qi = pl.program_id(1)
    m_sc[...] = jnp.full_like(m_sc, -jnp.inf)
    acc_sc[...] = jnp.zeros_like(acc_sc)

    def block(j, masked):
        kblk = kc_ref[0, pl.ds(pl.multiple_of(j * tq, tq), tq), :]
        for h in range(MLA_HEADS):
            rows = slice(h * tq, (h + 1) * tq)
            s = _dot_nt(q_ref[0, h], kblk)
            if masked:
                qpos = lax.broadcasted_iota(jnp.int32, (tq, tq), 0)
                kpos = lax.broadcasted_iota(jnp.int32, (tq, tq), 1)
                s = jnp.where(kpos <= qpos, s, -jnp.inf)
            m_prev = m_sc[rows, :]
            m_new = jnp.maximum(m_prev, jnp.broadcast_to(jnp.max(s, axis=-1, keepdims=True), (tq, LANES)))
            alpha = jnp.exp(m_prev - m_new)
            p = jnp.exp(s - jnp.concatenate([m_new] * (tq // LANES), axis=1))
            acc_sc[rows, :] = jnp.concatenate([alpha, alpha], axis=1) * acc_sc[rows, :] + _dot(p.astype(bf16), kblk)
            m_sc[rows, :] = m_new

    def body(j, carry):
        block(j, False)
        return carry

    lax.fori_loop(0, qi, body, 0)
    block(qi, True)
    for h in range(MLA_HEADS):
        acc = acc_sc[h * tq:(h + 1) * tq, :]
        o_ref[0, :, LANES * h:LANES * (h + 1)] = (acc[:, :LANES] / acc[:, 2 * LANES - 1:2 * LANES]).astype(bf16)


def _mla_prompt_call(q, kc):
    B, H, T, W = q.shape
    tq = min(ATT_TQ, T)
    R = H * tq
    return pl.pallas_call(
        functools.partial(_mla_prompt_kernel, tq=tq),
        out_shape=jax.ShapeDtypeStruct((B, T, H * LANES), bf16),
        grid=(B, T // tq),
        in_specs=[
            pl.BlockSpec((1, H, tq, W), lambda b, i: (b, 0, i, 0)),
            pl.BlockSpec((1, T, W), lambda b, i: (b, 0, 0)),
        ],
        out_specs=pl.BlockSpec((1, tq, H * LANES), lambda b, i: (b, i, 0)),
        scratch_shapes=[pltpu.VMEM((R, LANES), f32), pltpu.VMEM((R, W), f32)],
        compiler_params=_cparams("parallel", "parallel"),
        name="mla_prompt",
    )(q, kc)


def _mla_sample_kernel(pt_ref, q_ref, kc_ref, lat_hbm, krt_hbm, o_ref, lat_buf, krt_buf, sem, *, T, layer):
    b = pl.program_id(0)
    n_pages = pt_ref.shape[1]
    R = MLA_HEADS * T
    slot = b % 2

    def page_copies(seq, dst, pg):
        page = pt_ref[seq, pg]
        span = pl.ds(pl.multiple_of(pg * PAGE_SIZE, PAGE_SIZE), PAGE_SIZE)
        return (pltpu.make_async_copy(lat_hbm.at[layer, page], lat_buf.at[dst, span, :], sem.at[dst, 0]),
                pltpu.make_async_copy(krt_hbm.at[layer, page], krt_buf.at[dst, :, span], sem.at[dst, 1]))

    def start_pages(seq, dst):
        def body(pg, carry):
            for cp in page_copies(seq, dst, pg):
                cp.start()
            return carry
        lax.fori_loop(0, n_pages, body, 0, unroll=8)

    @pl.when(b == 0)
    def _():
        start_pages(0, 0)

    @pl.when(b + 1 < pl.num_programs(0))
    def _():
        start_pages(b + 1, 1 - slot)

    pltpu.make_async_copy(lat_buf.at[1 - slot], lat_buf.at[slot], sem.at[slot, 0]).wait()
    pltpu.make_async_copy(krt_buf.at[1 - slot], krt_buf.at[slot], sem.at[slot, 1]).wait()

    q = q_ref[0].reshape(R, 2 * LANES)
    kc = kc_ref[0]
    qpos = lax.broadcasted_iota(jnp.int32, (R, T), 0) % T
    kpos = lax.broadcasted_iota(jnp.int32, (R, T), 1)
    s_new = jnp.where(kpos <= qpos, _dot_nt(q, kc), -jnp.inf)
    span = n_pages * PAGE_SIZE // SAMPLE_KEY_SPANS
    parts = []
    for i in range(SAMPLE_KEY_SPANS):
        lat = lat_buf[slot, i * span:(i + 1) * span, :].astype(bf16)
        krt = krt_buf[slot, :, i * span:(i + 1) * span].astype(bf16)
        s = _dot_nt(q[:, :LANES], lat) + _dot(q[:, LANES:LANES + MLA_ROPE], krt)
        m = jnp.max(s, axis=-1, keepdims=True)
        p = jnp.exp(s - m)
        parts.append((m, jnp.sum(p, axis=-1, keepdims=True), _dot(p.astype(bf16), lat)))
    m_new = jnp.max(s_new, axis=-1, keepdims=True)
    p_new = jnp.exp(s_new - m_new)
    parts.append((m_new, jnp.sum(p_new, axis=-1, keepdims=True), _dot(p_new.astype(bf16), kc[:, :LANES])))
    m = functools.reduce(jnp.maximum, [pm for pm, _, _ in parts])
    l = sum(jnp.exp(pm - m) * pl_ for pm, pl_, _ in parts)
    o = (sum(jnp.exp(pm - m) * po for pm, _, po in parts) / l).astype(bf16)
    for h in range(MLA_HEADS):
        o_ref[0, :, LANES * h:LANES * (h + 1)] = o[h * T:(h + 1) * T]


def _mla_sample_call(q, kc, cache_lat, cache_krt, page_table, layer):
    B, H, T, W = q.shape
    n_keys = page_table.shape[1] * PAGE_SIZE
    grid_spec = pltpu.PrefetchScalarGridSpec(
        num_scalar_prefetch=1,
        grid=(B,),
        in_specs=[
            pl.BlockSpec((1, H, T, W), lambda b, pt: (b, 0, 0, 0)),
            pl.BlockSpec((1, T, W), lambda b, pt: (b, 0, 0)),
            pl.BlockSpec(memory_space=pl.ANY),
            pl.BlockSpec(memory_space=pl.ANY),
        ],
        out_specs=pl.BlockSpec((1, T, H * LANES), lambda b, pt: (b, 0, 0)),
        scratch_shapes=[
            pltpu.VMEM((2, n_keys, MLA_KV_RANK), f32),
            pltpu.VMEM((2, MLA_ROPE, n_keys), f32),
            pltpu.SemaphoreType.DMA((2, 2)),
        ],
    )
    return pl.pallas_call(
        functools.partial(_mla_sample_kernel, T=T, layer=layer),
        out_shape=jax.ShapeDtypeStruct((B, T, H * LANES), bf16),
        grid_spec=grid_spec,
        compiler_params=_cparams("arbitrary"),
        name="mla_sample",
    )(page_table, q, kc, cache_lat, cache_krt)


RET_LOG_GAMMA = tuple(float(np.log1p(-np.exp2(-5.0 - h))) for h in range(RET_HEADS))


def _ret_kernel(q_ref, k_ref, v_ref, s0_ref, *rest, nb, L):
    o_ref, sf_ref, s_sc = rest[-3:]
    c = pl.program_id(1)
    HK = RET_HEADS * RET_DK

    @pl.when(c == 0)
    def _():
        s_sc[...] = s0_ref[...].reshape(nb, HK, RET_DV)

    ti = lax.broadcasted_iota(jnp.int32, (L, L), 0)
    ji = lax.broadcasted_iota(jnp.int32, (L, L), 1)
    causal = ti >= ji
    dist = (ti - ji).astype(f32)
    tcol = lax.broadcasted_iota(jnp.int32, (L, 1), 0).astype(f32)
    lane_head = lax.broadcasted_iota(jnp.int32, (1, HK), 1) // RET_DK
    row_head = lax.broadcasted_iota(jnp.int32, (HK, 1), 0) // RET_DK
    decay = [jnp.exp(jnp.where(causal, dist * lg, -jnp.inf)) for lg in RET_LOG_GAMMA]
    grow = [jnp.exp((tcol + 1.0) * lg) for lg in RET_LOG_GAMMA]
    tail = [jnp.exp((L - 1.0 - tcol) * lg) for lg in RET_LOG_GAMMA]
    row_decay = jnp.zeros((HK, 1), f32)
    for h, lg in enumerate(RET_LOG_GAMMA):
        row_decay = jnp.where(row_head == h, math.exp(L * lg), row_decay)
    for sq in range(nb):
        q, k, v = q_ref[sq], k_ref[sq], v_ref[sq]
        kb = k.astype(bf16)
        s = s_sc[sq]
        sb = s.astype(bf16)
        upd = jnp.zeros((HK, RET_DV), f32)
        for h in range(RET_HEADS):
            mh = lane_head == h
            qh = jnp.where(mh, q, 0.0)
            p = (_dot_nt(qh.astype(bf16), kb) * decay[h]).astype(bf16)
            vh = v[:, RET_DV * h:RET_DV * (h + 1)].astype(bf16)
            o_ref[sq, :, RET_DV * h:RET_DV * (h + 1)] = _dot(p, vh) + _dot((qh * grow[h]).astype(bf16), sb)
            upd = upd + _dot_tn((jnp.where(mh, k, 0.0) * tail[h]).astype(bf16), vh)
        s_sc[sq] = row_decay * s + upd

    @pl.when(c == pl.num_programs(1) - 1)
    def _():
        sf_ref[...] = s_sc[...].reshape(nb, RET_HEADS, RET_DK, RET_DV)


def _stacked_alias(in_specs, args, stacked):
    in_specs.append(pl.BlockSpec(memory_space=pl.ANY))
    args.append(stacked)
    return {len(args) - 1: 1}


def _ret_call(rq, rk, rv, s0_all, layer, stacked):
    B, T, _ = rq.shape
    L = min(SCAN_CHUNK, T)
    nb = min(_row_tiles(B, T, SCAN_ROWS)[0], SCAN_SEQS)
    HK, HV = RET_HEADS * RET_DK, RET_HEADS * RET_DV
    row = lambda b, c: (b, c, 0)
    st = pl.BlockSpec((None, nb, RET_HEADS, RET_DK, RET_DV), lambda b, c: (layer, b, 0, 0, 0))
    in_specs = [pl.BlockSpec((nb, L, HK), row), pl.BlockSpec((nb, L, HK), row), pl.BlockSpec((nb, L, HV), row), st]
    args = [rq, rk, rv, s0_all]
    aliases = _stacked_alias(in_specs, args, stacked)
    return pl.pallas_call(
        functools.partial(_ret_kernel, nb=nb, L=L),
        out_shape=(jax.ShapeDtypeStruct((B, T, HV), f32), jax.ShapeDtypeStruct(s0_all.shape, f32)),
        grid=(B // nb, T // L),
        in_specs=in_specs,
        out_specs=(pl.BlockSpec((nb, L, HV), row), st),
        scratch_shapes=[pltpu.VMEM((nb, HK, RET_DV), f32)],
        input_output_aliases=aliases,
        compiler_params=_cparams("parallel", "arbitrary"),
        name="retention",
    )(*args)


def _head_rms(x, g, width):
    return jnp.concatenate([_rms(x[:, i:i + width], g) for i in range(0, x.shape[1], width)], axis=1)


def _even_post_kernel(x_ref, ada_ref, ol_ref, ro_ref, rg_ref, wuv_ref, rn_ref, wo_ref, g_ref, b_ref, o_ref):
    for bs, ts in _halves(*x_ref.shape[:2]):
        R = (bs.stop - bs.start) * (ts.stop - ts.start)
        rd = lambda ref: ref[bs, ts, :].reshape(R, ref.shape[2])
        mla_out = _dot(rd(ol_ref), wuv_ref[...])
        ret = _silu(rd(rg_ref).astype(f32)) * _head_rms(rd(ro_ref), rn_ref[...], RET_DV)
        cat = jnp.concatenate([mla_out, ret], axis=1).astype(bf16)
        _mixer_residual_ln(x_ref, ada_ref, o_ref, bs, ts, _dot(cat, wo_ref[...]), g_ref, b_ref)


def _even_post_call(x, ada, o_lat, ro, rg, wts, ln_g, ln_b, layer):
    B, T, D = x.shape
    bB, bT = _row_tiles(B, T, MIX_ROWS)
    row = lambda b, t: (b, t, 0)
    full = lambda shape: pl.BlockSpec(shape, lambda b, t: (0,) * len(shape))
    ln = pl.BlockSpec((None, None, 1, D), lambda b, t: (layer, 1, 0, 0))
    return pl.pallas_call(
        _even_post_kernel,
        out_shape=jax.ShapeDtypeStruct((B, T, D), f32),
        grid=(B // bB, T // bT),
        in_specs=[
            pl.BlockSpec((bB, bT, D), row), pl.BlockSpec((bB, 9, D), lambda b, t: (b, 0, 0)),
            pl.BlockSpec((bB, bT, MLA_HEADS * LANES), row), pl.BlockSpec((bB, bT, 512), row),
            pl.BlockSpec((bB, bT, 512), row),
            full((MLA_HEADS * LANES, MLA_HEADS * MLA_V)), full((1, RET_DV)), full((1024, D)), ln, ln,
        ],
        out_specs=pl.BlockSpec((bB, bT, D), row),
        compiler_params=_cparams("parallel", "parallel"),
        name="even_post",
    )(x, ada, o_lat, ro, rg, wts["w_uv_bd"], wts["ret_norm"], wts["w_out"], ln_g, ln_b)


ODD_COLS = 4352
ODD_OFF = dict(gq=0, gk=256, gv=512, glr=1024, gr=1152, z=1664, xbc=2688, dt=4224)


def _odd_pre_kernel(x_ref, ada_ref, win_ref, wg_ref, bg_ref, cw_ref, cb_ref, dtb_ref, c0_ref,
                    gq_ref, gk_ref, gv_ref, la_ref, gr_ref, z_ref, xs_ref, bm_ref, cm_ref, dt_ref, cs_ref, buf):
    bB, bT, D = x_ref.shape
    t = pl.program_id(1)
    O = ODD_OFF
    W = SSM_CONV - 1
    lo = SUBLANES - W

    @pl.when(t == 0)
    def _():
        buf[:, lo:SUBLANES, :] = c0_ref[...]

    cw = cw_ref[...]
    for bs, ts in _halves(bB, bT):
        pb, pt = bs.stop - bs.start, ts.stop - ts.start
        R = pb * pt
        x = x_ref[bs, ts, :]
        h = (x * (1.0 + ada_ref[bs, 4:5, :]) + ada_ref[bs, 3:4, :]).reshape(R, D).astype(bf16)
        proj = _dot(h, win_ref[...])
        sl = lambda name, w: proj[:, O[name]:O[name] + w]
        gq_ref[bs, ts, :] = (sl("gq", 256) * GLA_DK ** -0.5).reshape(pb, pt, 256)
        gk_ref[bs, ts, :] = sl("gk", 256).reshape(pb, pt, 256)
        gv_ref[bs, ts, :] = sl("gv", 512).astype(bf16).reshape(pb, pt, 512)
        gr_ref[bs, ts, :] = sl("gr", 512).astype(bf16).reshape(pb, pt, 512)
        z_ref[bs, ts, :] = sl("z", 1024).astype(bf16).reshape(pb, pt, 1024)
        gate_pre = _dot(sl("glr", LANES).astype(bf16), wg_ref[...]) + bg_ref[...]
        la_ref[bs, ts, :] = (_log_sigmoid(gate_pre) / GLA_TAU).reshape(pb, pt, 256)
        dt_ref[bs, ts, :] = _softplus(sl("dt", SSM_HEADS) + dtb_ref[...]).reshape(pb, pt, SSM_HEADS)
        buf[bs, SUBLANES + ts.start:SUBLANES + ts.stop, :] = sl("xbc", SSM_CONV_DIM).reshape(pb, pt, SSM_CONV_DIM)
        conv = cb_ref[...] + sum(buf[bs, lo + i + ts.start:lo + i + ts.stop, :] * cw[i:i + 1, :]
                                 for i in range(SSM_CONV))
        act = _silu(conv)
        xs_ref[bs, ts, :] = act[:, :, :SSM_INNER].astype(bf16)
        bm_ref[bs, ts, :] = act[:, :, SSM_INNER:SSM_INNER + 256].astype(bf16)
        cm_ref[bs, ts, :] = act[:, :, SSM_INNER + 256:].astype(bf16)
    tail = buf[:, bT + lo:bT + SUBLANES, :]
    buf[:, lo:SUBLANES, :] = tail

    @pl.when(t == pl.num_programs(1) - 1)
    def _():
        cs_ref[...] = tail


def _odd_pre_call(x, ada, wts, conv0_all, layer):
    B, T, D = x.shape
    bB, bT = _row_tiles(B, T, MIX_ROWS)
    row = lambda b, t: (b, t, 0)
    full = lambda shape: pl.BlockSpec(shape, lambda b, t: (0,) * len(shape))
    blk = lambda w: pl.BlockSpec((bB, bT, w), row)
    widths = (256, 256, 512, 256, 512, 1024, SSM_INNER, 256, 256, SSM_HEADS)
    dtypes = (f32, f32, bf16, f32, bf16, bf16, bf16, bf16, bf16, f32)
    cs = pl.BlockSpec((bB, SSM_CONV - 1, SSM_CONV_DIM), lambda b, t: (b, 0, 0))
    cs_in = pl.BlockSpec((None, bB, SSM_CONV - 1, SSM_CONV_DIM), lambda b, t: (layer, b, 0, 0))
    return pl.pallas_call(
        _odd_pre_kernel,
        out_shape=tuple(jax.ShapeDtypeStruct((B, T, w), dt) for w, dt in zip(widths, dtypes))
        + (jax.ShapeDtypeStruct(conv0_all.shape[1:], f32),),
        grid=(B // bB, T // bT),
        in_specs=[
            blk(D), pl.BlockSpec((bB, 9, D), lambda b, t: (b, 0, 0)),
            full((D, ODD_COLS)), full((LANES, 256)), full((1, 256)),
            full((SSM_CONV, SSM_CONV_DIM)), full((1, SSM_CONV_DIM)), full((1, SSM_HEADS)), cs_in,
        ],
        out_specs=tuple(blk(w) for w in widths) + (cs,),
        scratch_shapes=[pltpu.VMEM((bB, SUBLANES + bT, SSM_CONV_DIM), f32)],
        compiler_params=_cparams("parallel", "arbitrary"),
        name="odd_pre",
    )(x, ada, wts["w_in"], wts["w_gate"], wts["b_gate"], wts["conv_w"], wts["conv_b"], wts["dt_bias"], conv0_all)


def _gla_chunk(q, k, la, v, s, sel, o_ref, sq, r_base, *, L, sb):
    HK, HV = GLA_HEADS * GLA_DK, GLA_HEADS * GLA_DV
    row_head = lax.broadcasted_iota(jnp.int32, (HK, HV), 0) // GLA_DK
    col_head = lax.broadcasted_iota(jnp.int32, (HK, HV), 1) // GLA_DV
    lane_hk = lax.broadcasted_iota(jnp.int32, (1, HK), 1) // GLA_DK
    lane_hv = lax.broadcasted_iota(jnp.int32, (1, HV), 1) // GLA_DV
    vb = v.astype(bf16)
    v = v.astype(f32)
    ti = lax.broadcasted_iota(jnp.int32, (L, L), 0)
    ji = lax.broadcasted_iota(jnp.int32, (L, L), 1)
    seg = _dot((ti >= ji).astype(f32), la, precision=HIGHEST)
    o_inter = _dot((q * jnp.exp(seg)).astype(bf16), s.astype(bf16))
    j3 = lax.broadcasted_iota(jnp.int32, (sb, sb, 1), 0)
    t3 = lax.broadcasted_iota(jnp.int32, (sb, sb, 1), 1)
    for i in range(L // sb):
        r0 = i * sb
        qi, ki, si, vi = q[r0:r0 + sb], k[r0:r0 + sb], seg[r0:r0 + sb], v[r0:r0 + sb]
        diff = si[None, :, :] - si[:, None, :]
        e = qi[None, :, :] * ki[:, None, :] * jnp.exp(jnp.where(t3 >= j3, diff, -jnp.inf))
        sc = _dot(e.reshape(sb * sb, HK).astype(bf16), sel)
        o = o_inter[r0:r0 + sb] + jnp.sum(sc.reshape(sb, sb, HV) * vi[:, None, :], axis=0)
        if i > 0:
            segp = seg[r0 - 1:r0]
            qt = qi * jnp.exp(si - segp)
            kt = k[:r0] * jnp.exp(segp - seg[:r0])
            qs = jnp.concatenate([jnp.where(lane_hk == h, qt, 0.0) for h in range(GLA_HEADS)], axis=0)
            pr = _dot_nt(qs.astype(bf16), kt.astype(bf16)).astype(bf16)
            full = _dot(pr, vb[:r0])
            for h in range(GLA_HEADS):
                o = o + jnp.where(lane_hv == h, full[h * sb:(h + 1) * sb], 0.0)
        o_ref[sq, r_base + r0:r_base + r0 + sb, :] = o
    seg_last = seg[L - 1:L]
    kt = (k * jnp.exp(seg_last - seg)).astype(bf16)
    upd = _dot_tn(kt, vb)
    decay_col = jnp.transpose(jnp.broadcast_to(jnp.exp(seg_last), (SUBLANES, HK)))[:, 0:1]
    return jnp.where(row_head == col_head, decay_col * s + upd, 0.0)


def _gla_kernel(q_ref, k_ref, la_ref, v_ref, s0_ref, sel_ref, *rest, nb, nc, L, sb):
    o_ref, sf_ref, s_sc = rest[-3:]
    c = pl.program_id(1)

    @pl.when(c == 0)
    def _():
        s_sc[...] = jnp.zeros_like(s_sc)
        for sq in range(nb):
            for h in range(GLA_HEADS):
                s_sc[sq, GLA_DK * h:GLA_DK * (h + 1), GLA_DV * h:GLA_DV * (h + 1)] = s0_ref[sq, h]

    sel = sel_ref[...]
    for sq in range(nb):
        s = s_sc[sq]
        for ci in range(nc):
            rows = slice(ci * L, (ci + 1) * L)
            s = _gla_chunk(q_ref[sq, rows, :], k_ref[sq, rows, :], la_ref[sq, rows, :], v_ref[sq, rows, :], s, sel,
                           o_ref, sq, ci * L, L=L, sb=sb)
        s_sc[sq] = s

    @pl.when(c == pl.num_programs(1) - 1)
    def _():
        for sq in range(nb):
            for h in range(GLA_HEADS):
                sf_ref[sq, h] = s_sc[sq, GLA_DK * h:GLA_DK * (h + 1), GLA_DV * h:GLA_DV * (h + 1)]


def _gla_call(gq, gk, la, gv, s0_all, sel, layer, stacked):
    B, T, _ = gq.shape
    L = min(GLA_CHUNK, T)
    sb = min(GLA_SUB, L)
    nb, Lb = _row_tiles(B, T, SCAN_ROWS)
    nb = min(nb, SCAN_SEQS)
    HK, HV = GLA_HEADS * GLA_DK, GLA_HEADS * GLA_DV
    row = lambda b, c: (b, c, 0)
    st = pl.BlockSpec((None, nb, GLA_HEADS, GLA_DK, GLA_DV), lambda b, c: (layer, b, 0, 0, 0))
    in_specs = [pl.BlockSpec((nb, Lb, HK), row)] * 3 + [pl.BlockSpec((nb, Lb, HV), row), st,
                pl.BlockSpec((HK, HV), lambda b, c: (0, 0))]
    args = [gq, gk, la, gv, s0_all, sel]
    aliases = _stacked_alias(in_specs, args, stacked)
    return pl.pallas_call(
        functools.partial(_gla_kernel, nb=nb, nc=Lb // L, L=L, sb=sb),
        out_shape=(jax.ShapeDtypeStruct((B, T, HV), f32), jax.ShapeDtypeStruct(s0_all.shape, f32)),
        grid=(B // nb, T // Lb),
        in_specs=in_specs,
        out_specs=(pl.BlockSpec((nb, Lb, HV), row), st),
        scratch_shapes=[pltpu.VMEM((nb, HK, HV), f32)],
        input_output_aliases=aliases,
        compiler_params=_cparams("parallel", "arbitrary"),
        name="gla",
    )(*args)


def _ssd_seq(xs_ref, bm_ref, cm_ref, dt_ref, a_row, d_ref, y_ref, s_sc, sq, *, L):
    P, N = SSM_HEADDIM, SSM_DSTATE
    pairs = SSM_HEADS // 2
    dt = dt_ref[sq]
    la = dt * a_row
    ti = lax.broadcasted_iota(jnp.int32, (L, L), 0)
    ji = lax.broadcasted_iota(jnp.int32, (L, L), 1)
    causal = ti >= ji
    seg = _dot(causal.astype(f32), la, precision=HIGHEST)
    seg_t = _dot_tn(la, (ti <= ji).astype(f32), precision=HIGHEST)
    shift_t = seg_t - jnp.log(dt.T)
    seg_last = seg[L - 1:L]
    grow = jnp.exp(seg)
    wtail = dt * jnp.exp(seg_last - seg)
    chunk_decay = jnp.exp(seg_last)
    lo = lax.broadcasted_iota(jnp.int32, (1, 2 * P), 1) < P
    row_lo = lax.broadcasted_iota(jnp.int32, (2 * P, 1), 0) < P
    for g in range(SSM_GROUPS):
        cg = cm_ref[sq, :, N * g:N * (g + 1)].astype(bf16)
        bg = bm_ref[sq, :, N * g:N * (g + 1)].astype(bf16)
        gmat = _dot_nt(cg, bg)
        for i in range(g * pairs // SSM_GROUPS, (g + 1) * pairs // SSM_GROUPS):
            ha, hb = 2 * i, 2 * i + 1
            cols = slice(2 * P * i, 2 * P * (i + 1))
            xb = xs_ref[sq, :, cols]
            xp = xb.astype(f32)
            zero = jnp.zeros_like(xb)
            pa = (gmat * jnp.exp(jnp.where(causal, seg[:, ha:ha + 1] - shift_t[ha:ha + 1, :], -jnp.inf))).astype(bf16)
            pb = (gmat * jnp.exp(jnp.where(causal, seg[:, hb:hb + 1] - shift_t[hb:hb + 1, :], -jnp.inf))).astype(bf16)
            y = _dot(pa, jnp.where(lo, xb, zero)) + _dot(pb, jnp.where(lo, zero, xb))
            sp = s_sc[sq, i]
            y = y + _dot_nt(cg, sp.astype(bf16)) * jnp.where(lo, grow[:, ha:ha + 1], grow[:, hb:hb + 1])
            y_ref[sq, :, cols] = y + d_ref[:, cols] * xp
            xw = (xp * jnp.where(lo, wtail[:, ha:ha + 1], wtail[:, hb:hb + 1])).astype(bf16)
            decay = jnp.where(row_lo, chunk_decay[:, ha:ha + 1], chunk_decay[:, hb:hb + 1])
            s_sc[sq, i] = decay * sp + _dot_tn(xw, bg)


def _ssd_kernel(xs_ref, bm_ref, cm_ref, dt_ref, a_ref, d_ref, s0_ref, *rest, nb, L):
    y_ref, sf_ref, s_sc = rest[-3:]
    c = pl.program_id(1)
    P, N = SSM_HEADDIM, SSM_DSTATE
    pairs = SSM_HEADS // 2

    @pl.when(c == 0)
    def _():
        s_sc[...] = s0_ref[...].reshape(nb, pairs, 2 * P, N)

    a_row = a_ref[...]
    for sq in range(nb):
        _ssd_seq(xs_ref, bm_ref, cm_ref, dt_ref, a_row, d_ref, y_ref, s_sc, sq, L=L)

    @pl.when(c == pl.num_programs(1) - 1)
    def _():
        sf_ref[...] = s_sc[...].reshape(nb, SSM_HEADS, P, N)


def _ssd_call(xs, bm, cm, dt, a_row, d_row, s0_t, layer, stacked):
    B, T, _ = xs.shape
    L = min(SCAN_CHUNK, T)
    nb = min(_row_tiles(B, T, SCAN_ROWS)[0], SSD_SEQS)
    row = lambda b, c: (b, c, 0)
    st_shape = (SSM_HEADS, SSM_HEADDIM, SSM_DSTATE)
    st = pl.BlockSpec((None, nb) + st_shape, lambda b, c: (layer, b, 0, 0, 0))
    in_specs = [
        pl.BlockSpec((nb, L, SSM_INNER), row), pl.BlockSpec((nb, L, 256), row), pl.BlockSpec((nb, L, 256), row),
        pl.BlockSpec((nb, L, SSM_HEADS), row),
        pl.BlockSpec((1, SSM_HEADS), lambda b, c: (0, 0)), pl.BlockSpec((1, SSM_INNER), lambda b, c: (0, 0)), st,
    ]
    args = [xs, bm, cm, dt, a_row, d_row, s0_t]
    aliases = _stacked_alias(in_specs, args, stacked)
    return pl.pallas_call(
        functools.partial(_ssd_kernel, nb=nb, L=L),
        out_shape=(jax.ShapeDtypeStruct((B, T, SSM_INNER), f32), jax.ShapeDtypeStruct(s0_t.shape, f32)),
        grid=(B // nb, T // L),
        in_specs=in_specs,
        out_specs=(pl.BlockSpec((nb, L, SSM_INNER), row), st),
        scratch_shapes=[pltpu.VMEM((nb, SSM_HEADS // 2, 2 * SSM_HEADDIM, SSM_DSTATE), f32)],
        input_output_aliases=aliases,
        compiler_params=_cparams("parallel", "arbitrary"),
        name="ssd",
    )(*args)


def _odd_post_kernel(x_ref, ada_ref, go_ref, gr_ref, y_ref, z_ref, gn_ref, sn_ref, wo_ref, g_ref, b_ref, o_ref):
    gw = SSM_INNER // SSM_GROUPS
    sn = sn_ref[...]
    for bs, ts in _halves(*x_ref.shape[:2]):
        R = (bs.stop - bs.start) * (ts.stop - ts.start)
        rd = lambda ref: ref[bs, ts, :].reshape(R, ref.shape[2])
        gla = _silu(rd(gr_ref).astype(f32)) * _head_rms(rd(go_ref), gn_ref[...], GLA_DV)
        yz = rd(y_ref) * _silu(rd(z_ref).astype(f32))
        yn = jnp.concatenate([_rms(yz[:, gw * g:gw * (g + 1)], sn[:, gw * g:gw * (g + 1)])
                              for g in range(SSM_GROUPS)], axis=1)
        cat = jnp.concatenate([gla, yn], axis=1).astype(bf16)
        _mixer_residual_ln(x_ref, ada_ref, o_ref, bs, ts, _dot(cat, wo_ref[...]), g_ref, b_ref)


def _odd_post_call(x, ada, go, gr, y, z, wts, ln_g, ln_b, layer):
    B, T, D = x.shape
    bB, bT = _row_tiles(B, T, MIX_ROWS)
    row = lambda b, t: (b, t, 0)
    full = lambda shape: pl.BlockSpec(shape, lambda b, t: (0,) * len(shape))
    blk = lambda w: pl.BlockSpec((bB, bT, w), row)
    ln = pl.BlockSpec((None, None, 1, D), lambda b, t: (layer, 1, 0, 0))
    return pl.pallas_call(
        _odd_post_kernel,
        out_shape=jax.ShapeDtypeStruct((B, T, D), f32),
        grid=(B // bB, T // bT),
        in_specs=[
            blk(D), pl.BlockSpec((bB, 9, D), lambda b, t: (b, 0, 0)),
            blk(512), blk(512), blk(SSM_INNER), blk(SSM_INNER),
            full((1, GLA_DV)), full((1, SSM_INNER)), full((512 + SSM_INNER, D)), ln, ln,
        ],
        out_specs=blk(D),
        compiler_params=_cparams("parallel", "parallel"),
        name="odd_post",
    )(x, ada, go, gr, y, z, wts["gla_norm"], wts["ssm_norm"], wts["w_out"], ln_g, ln_b)


def _pad_cols(w, width):
    return jnp.pad(w, ((0, 0), (0, width - w.shape[1])))


def _prep_even(j, w_in_even, mla_q_norm, w_uq, mla_kv_norm, w_uk, w_uv, ret_norm, w_out_even):
    w = w_in_even[j]
    k0 = MLA_Q_RANK + MLA_KV_RANK
    w_in = jnp.concatenate([w[:, :k0], _pad_cols(w[:, k0:k0 + MLA_ROPE], LANES), w[:, k0 + MLA_ROPE:]], axis=1)
    uq = w_uq[j].reshape(MLA_Q_RANK, MLA_HEADS, MLA_NOPE + MLA_ROPE)
    w_q_nope = uq[:, :, :MLA_NOPE].reshape(MLA_Q_RANK, MLA_HEADS * MLA_NOPE)
    w_q_rope = jnp.pad(uq[:, :, MLA_NOPE:], ((0, 0), (0, 0), (0, LANES - MLA_ROPE))).reshape(MLA_Q_RANK, MLA_HEADS * LANES)
    eye2 = jnp.eye(2, dtype=f32)
    uk = jnp.transpose(w_uk[j], (1, 2, 0)).reshape(MLA_HEADS // 2, 2, MLA_NOPE, MLA_KV_RANK)
    w_uk_bd = (uk[:, :, :, None, :] * eye2[None, :, None, :, None]).reshape(MLA_HEADS // 2, 2 * MLA_NOPE, 2 * MLA_KV_RANK)
    eye8 = jnp.eye(MLA_HEADS, dtype=f32)
    uv = jnp.transpose(w_uv[j], (1, 0, 2))
    w_uv_bd = (uv[:, :, None, :] * eye8[:, None, :, None]).reshape(MLA_HEADS * MLA_KV_RANK, MLA_HEADS * MLA_V)
    return dict(
        w_in=w_in.astype(bf16), q_norm=mla_q_norm[j][None, :], w_q_nope=w_q_nope.astype(bf16),
        w_q_rope=w_q_rope.astype(bf16), w_uk_bd=w_uk_bd.astype(bf16), kv_norm=mla_kv_norm[j][None, :],
        w_uv_bd=w_uv_bd.astype(bf16), ret_norm=ret_norm[j][None, :], w_out=w_out_even[j].astype(bf16),
    )


def _prep_odd(j, w_in_odd, gla_w_gate, gla_b_gate, gla_norm, ssm_conv_w, ssm_conv_b, ssm_dt_bias, ssm_a_log, ssm_d,
              ssm_norm, w_out_odd):
    w = w_in_odd[j]
    c_glr = 2 * GLA_HEADS * GLA_DK + GLA_HEADS * GLA_DV
    c_dt = w.shape[1] - SSM_HEADS
    w_in = jnp.concatenate([
        w[:, :c_glr], _pad_cols(w[:, c_glr:c_glr + GLA_GATE_RANK], LANES), w[:, c_glr + GLA_GATE_RANK:c_dt],
        _pad_cols(w[:, c_dt:], LANES)], axis=1)
    w_gate = jnp.pad(gla_w_gate[j], ((0, LANES - GLA_GATE_RANK), (0, 0)))
    return dict(
        w_in=w_in.astype(bf16), w_gate=w_gate.astype(bf16), b_gate=gla_b_gate[j][None, :],
        conv_w=ssm_conv_w[j], conv_b=ssm_conv_b[j][None, :], dt_bias=ssm_dt_bias[j][None, :],
        a_row=-jnp.exp(ssm_a_log[j].astype(f32))[None, :], d_row=jnp.repeat(ssm_d[j], SSM_HEADDIM)[None, :],
        gla_norm=gla_norm[j][None, :], ssm_norm=ssm_norm[j][None, :], w_out=w_out_odd[j].astype(bf16),
    )


def _gla_select():
    r = np.arange(GLA_HEADS * GLA_DK)[:, None] // GLA_DK
    c = np.arange(GLA_HEADS * GLA_DV)[None, :] // GLA_DV
    return jnp.asarray(r == c, dtype=bf16)


def _trunk(x, ada_all, pos, p, ret_s0, gla_s0, ssm_s0_t, conv0, cache_lat, cache_krt, page_table):
    B, T, D = x.shape
    bB, _ = _row_tiles(B, T, MIX_ROWS)
    tabs_m = _rope_tables(pos, MLA_ROPE // 2, LANES, MLA_ROPE)
    tabs_r = _rope_tables(pos, RET_DK // 2, RET_HEADS * RET_DK, RET_HEADS * RET_DK)
    if bB > 1:
        tabs_m = tuple(jnp.tile(a, (bB, 1)) for a in tabs_m)
        tabs_r = tuple(jnp.tile(a, (bB, 1)) for a in tabs_r)
    tabs = tabs_m + tabs_r
    sel = _gla_select()
    lat_l, kr_l, conv_l = [], [], []
    ret_stacked, gla_stacked, ssm_stacked = (jnp.zeros(a.shape, f32) for a in (ret_s0, gla_s0, ssm_s0_t))
    for l in range(DEPTH):
        ada = ada_all[l]
        j = l // 2
        x = _ffn_call(x, ada, p["ffn_w1"], p["ffn_w3"], p["ffn_w2"], p["ln_g"], p["ln_b"], l, 0)
        if l % 2 == 0:
            wts = p["even"][j]
            q, kc, lat, kr, rq, rk, rv, rg = _even_pre_call(x, ada, wts, tabs)
            if page_table is None:
                o_lat = _mla_prompt_call(q, kc)
            else:
                o_lat = _mla_sample_call(q, kc, cache_lat, cache_krt, page_table, j)
            ro, ret_stacked = _ret_call(rq, rk, rv, ret_s0, j, ret_stacked)
            x = _even_post_call(x, ada, o_lat, ro, rg, wts, p["ln_g"], p["ln_b"], l)
            lat_l.append(lat)
            kr_l.append(kr)
        else:
            wts = p["odd"][j]
            gq, gk, gv, la, gr, z, xs, bm, cm, dt, cs = _odd_pre_call(x, ada, wts, conv0, j)
            go, gla_stacked = _gla_call(gq, gk, la, gv, gla_s0, sel, j, gla_stacked)
            y, ssm_stacked = _ssd_call(xs, bm, cm, dt, wts["a_row"], wts["d_row"], ssm_s0_t, j, ssm_stacked)
            x = _odd_post_call(x, ada, go, gr, y, z, wts, p["ln_g"], p["ln_b"], l)
            conv_l.append(cs)
        x = _ffn_call(x, ada, p["ffn_w1"], p["ffn_w3"], p["ffn_w2"], p["ln_g"], p["ln_b"], l, 1)
    ssm_s = jnp.swapaxes(ssm_stacked, -1, -2)
    return x, jnp.stack(lat_l), jnp.stack(kr_l), ret_stacked, gla_stacked, ssm_s, jnp.stack(conv_l)


def kernel(x_prompt, x_sample, cache_mla_latent, cache_mla_krope, page_table, state_retention, state_gla, state_ssm, state_conv, c_prompt, c_sample, w_ada, b_ada, ln_g, ln_b, ffn_w1, ffn_w3, ffn_w2, w_in_even, mla_q_norm, w_uq, mla_kv_norm, w_uk, w_uv, ret_norm, w_out_even, w_in_odd, gla_w_gate, gla_b_gate, gla_norm, ssm_conv_w, ssm_conv_b, ssm_dt_bias, ssm_a_log, ssm_d, ssm_norm, w_out_odd):
    n_even, n_odd = (DEPTH + 1) // 2, DEPTH // 2
    bp, tp, D = x_prompt.shape
    bs, ts, _ = x_sample.shape
    p = dict(
        ffn_w1=ffn_w1.astype(bf16), ffn_w3=ffn_w3.astype(bf16), ffn_w2=ffn_w2.astype(bf16),
        ln_g=ln_g.reshape(DEPTH, 3, 1, D), ln_b=ln_b.reshape(DEPTH, 3, 1, D),
        even=[_prep_even(j, w_in_even, mla_q_norm, w_uq, mla_kv_norm, w_uk, w_uv, ret_norm, w_out_even)
              for j in range(n_even)],
        odd=[_prep_odd(j, w_in_odd, gla_w_gate, gla_b_gate, gla_norm, ssm_conv_w, ssm_conv_b, ssm_dt_bias,
                       ssm_a_log, ssm_d, ssm_norm, w_out_odd) for j in range(n_odd)],
    )
    ada_all = _ada_call(jnp.concatenate([c_prompt, c_sample], axis=0), w_ada, b_ada)
    ada_p = ada_all[:, :bp].reshape(DEPTH, bp, 9, D)
    ada_s = ada_all[:, bp:].reshape(DEPTH, bs, 9, D)
    past_len = page_table.shape[1] * PAGE_SIZE
    pos_p = jnp.arange(tp, dtype=jnp.int32)
    pos_s = past_len + jnp.arange(ts, dtype=jnp.int32)
    ret0 = jnp.zeros((n_even, bp) + state_retention.shape[2:], f32)
    gla0 = jnp.zeros((n_odd, bp) + state_gla.shape[2:], f32)
    ssm0_t = jnp.zeros((n_odd, bp, SSM_HEADS, SSM_HEADDIM, SSM_DSTATE), f32)
    conv0 = jnp.zeros((n_odd, bp) + state_conv.shape[2:], f32)
    out_p = _trunk(x_prompt, ada_p, pos_p, p, ret0, gla0, ssm0_t, conv0, None, None, None)
    out_s = _trunk(x_sample, ada_s, pos_s, p, state_retention, state_gla, jnp.swapaxes(state_ssm, -1, -2), state_conv,
                   cache_mla_latent, jnp.swapaxes(cache_mla_krope, -1, -2), page_table)
    return tuple(a for pair in zip(out_p, out_s) for a in pair)
```
